```python
import math
import jax
import jax.numpy as jnp
from jax import lax
import numpy as np

D_MODEL = 1024
BATCH = 8
SEQ = 2048
DEPTH = 4
DEC_BATCH = 32
DEC_SEQ = 1
PAST_LEN = 8192
PAGE_SIZE = 128

HEAD_DIM = 64
N_BRANCH = 4
BRANCH_WIDTH = D_MODEL // N_BRANCH
H_MOBA = BRANCH_WIDTH // HEAD_DIM
H_FOX = BRANCH_WIDTH // HEAD_DIM
H_GDN = BRANCH_WIDTH // HEAD_DIM
MOBA_BLOCK = 256
MOBA_TOPK = 3
MOBA_Q_CHUNK = 32
FOX_Q_CHUNK = 128
S5_GROUP = 16
S5_GROUPS = BRANCH_WIDTH // S5_GROUP
S5_STATE = 64
GDN_CONV = 4
GDN_CHUNK = 64
D_FF = 4 * D_MODEL
N_ADA = 6
EPS = 1e-6
NEG_INF = -1e30

COLS_MOBA = 3 * BRANCH_WIDTH
COLS_FOX = 3 * BRANCH_WIDTH + H_FOX
COLS_S5 = BRANCH_WIDTH
COLS_GDN = 4 * BRANCH_WIDTH + 2 * H_GDN
COLS_GATE = N_BRANCH * D_MODEL
IN_SPLITS = (COLS_MOBA, COLS_MOBA + COLS_FOX, COLS_MOBA + COLS_FOX + COLS_S5, COLS_MOBA + COLS_FOX + COLS_S5 + COLS_GDN)
IN_COLS = IN_SPLITS[-1] + COLS_GATE

STATE_KEYS = ('moba_k', 'moba_v', 'fox_k', 'fox_v', 'fox_logf', 's5_re', 's5_im', 'gdn', 'gdn_conv')

kernel_name = 'hybrid_moba_fox_s5_gdn_decoder_step'


def rms_norm(x, g):
    xf = x.astype(jnp.float32)
    y = xf * lax.rsqrt(jnp.mean(jnp.square(xf), axis=-1, keepdims=True) + EPS)
    return (y * g.astype(jnp.float32)).astype(x.dtype)


def l2_norm(x):
    xf = x.astype(jnp.float32)
    return xf * lax.rsqrt(jnp.sum(jnp.square(xf), axis=-1, keepdims=True) + EPS)


def alibi_slopes(n):
    return jnp.asarray(np.exp2(-8.0 * np.arange(1, n + 1) / n), dtype=jnp.float32)


def split_heads(t):
    return t.reshape(t.shape[0], t.shape[1], -1, HEAD_DIM)


def gather_pages(pool, layer, page_table):
    rows = pool[layer, page_table]
    return rows.reshape((rows.shape[0], rows.shape[1] * rows.shape[2]) + rows.shape[3:])


def moba_attention(q, k, v):
    B, Sq, H, d = q.shape
    L = k.shape[1]
    nb = -(-L // MOBA_BLOCK)
    pad = nb * MOBA_BLOCK - L
    kp = jnp.pad(k, ((0, 0), (0, pad), (0, 0), (0, 0)))
    vp = jnp.pad(v, ((0, 0), (0, pad), (0, 0), (0, 0)))
    k_blocks = kp.reshape(B, nb, MOBA_BLOCK, H, d).transpose(0, 3, 1, 2, 4)
    v_blocks = vp.reshape(B, nb, MOBA_BLOCK, H, d).transpose(0, 3, 1, 2, 4)
    k_mean = jnp.mean(k_blocks.astype(jnp.float32), axis=3)
    topk = min(MOBA_TOPK, nb)
    qc = math.gcd(Sq, MOBA_Q_CHUNK)
    n = Sq // qc
    qh = q.reshape(B, n, qc, H, d).transpose(1, 0, 3, 2, 4)
    qpos = ((L - Sq) + jnp.arange(Sq, dtype=jnp.int32)).reshape(n, qc)
    slopes = alibi_slopes(H)
    scale = HEAD_DIM ** -0.5
    blk_ids = jnp.arange(nb, dtype=jnp.int32)
    offs = jnp.arange(MOBA_BLOCK, dtype=jnp.int32)
    b_ids = jnp.arange(B)[:, None, None, None]
    h_ids = jnp.arange(H)[None, :, None, None]

    def one(args):
        qi, pi = args
        own = pi // MOBA_BLOCK
        gate = jnp.einsum('bhqd,bhnd->bhqn', qi.astype(jnp.float32), k_mean)
        gate = jnp.where(blk_ids[None, :] < own[:, None], gate, NEG_INF)
        _, sel = lax.top_k(gate, topk)
        idx = jnp.concatenate([sel, jnp.broadcast_to(own[:, None], (B, H, qc, 1))], axis=-1)
        slot_ok = jnp.concatenate([sel < own[:, None], jnp.ones((B, H, qc, 1), dtype=bool)], axis=-1)
        kg = k_blocks[b_ids, h_ids, idx]
        vg = v_blocks[b_ids, h_ids, idx]
        dist = pi[:, None, None] - (idx[..., None] * MOBA_BLOCK + offs)
        s = jnp.einsum('bhqd,bhqnkd->bhqnk', qi, kg).astype(jnp.float32) * scale
        s = s - slopes[:, None, None, None] * dist.astype(jnp.float32)
        s = jnp.where(slot_ok[..., None] & (dist >= 0), s, NEG_INF)
        p = jax.nn.softmax(s.reshape(B, H, qc, -1), axis=-1).reshape(s.shape)
        return jnp.einsum('bhqnk,bhqnkd->bhqd', p.astype(vg.dtype), vg)

    o = lax.map(one, (qh, qpos))
    return o.transpose(1, 0, 3, 2, 4).reshape(B, Sq, H, d)


def fox_attention(q, k, v, fq, fk):
    B, Sq, H, d = q.shape
    L = k.shape[1]
    qc = math.gcd(Sq, FOX_Q_CHUNK)
    n = Sq // qc
    qh = q.reshape(B, n, qc, H, d).transpose(1, 0, 3, 2, 4)
    fqh = fq.reshape(B, n, qc, H).transpose(1, 0, 3, 2)
    qpos = ((L - Sq) + jnp.arange(Sq, dtype=jnp.int32)).reshape(n, qc)
    kh = k.transpose(0, 2, 1, 3)
    vh = v.transpose(0, 2, 1, 3)
    fkh = fk.transpose(0, 2, 1)
    kpos = jnp.arange(L, dtype=jnp.int32)
    scale = HEAD_DIM ** -0.5

    def one(args):
        qi, fi, pi = args
        s = jnp.einsum('bhqd,bhkd->bhqk', qi, kh).astype(jnp.float32) * scale + fi[..., None] - fkh[:, :, None, :]
        s = jnp.where(pi[:, None] >= kpos[None, :], s, NEG_INF)
        p = jax.nn.softmax(s, axis=-1)
        return jnp.einsum('bhqk,bhkd->bhqd', p.astype(vh.dtype), vh)

    o = lax.map(one, (qh, fqh, qpos))
    return o.transpose(1, 0, 3, 2, 4).reshape(B, Sq, H, d)


def complex_affine_combine(x, y):
    a1r, a1i, b1r, b1i = x
    a2r, a2i, b2r, b2i = y
    return (a2r * a1r - a2i * a1i, a2r * a1i + a2i * a1r,
            a2r * b1r - a2i * b1i + b2r, a2r * b1i + a2i * b1r + b2i)


def s5_ssm(u, h_re, h_im, lam_re, lam_im, b_re, b_im, c_re, c_im, d_skip, log_dt):
    B, S, W = u.shape
    f32 = jnp.float32
    uf = u.astype(f32).reshape(B, S, S5_GROUPS, S5_GROUP)
    lam_re = lam_re.astype(f32)
    lam_im = lam_im.astype(f32)
    dt = jnp.exp(log_dt.astype(f32))[:, None]
    mag = jnp.exp(lam_re * dt)
    lb_re = mag * jnp.cos(lam_im * dt)
    lb_im = mag * jnp.sin(lam_im * dt)
    den = jnp.square(lam_re) + jnp.square(lam_im)
    nr = lb_re - 1.0
    f_re = (nr * lam_re + lb_im * lam_im) / den
    f_im = (lb_im * lam_re - nr * lam_im) / den
    b_re = b_re.astype(f32)
    b_im = b_im.astype(f32)
    bb_re = f_re[..., None] * b_re - f_im[..., None] * b_im
    bb_im = f_re[..., None] * b_im + f_im[..., None] * b_re
    bu_re = jnp.einsum('bsgh,gph->bsgp', uf, bb_re)
    bu_im = jnp.einsum('bsgh,gph->bsgp', uf, bb_im)
    h_re = h_re.astype(f32)
    h_im = h_im.astype(f32)
    bu_re = bu_re.at[:, 0].add(lb_re * h_re - lb_im * h_im)
    bu_im = bu_im.at[:, 0].add(lb_re * h_im + lb_im * h_re)
    a_re = jnp.broadcast_to(lb_re, bu_re.shape)
    a_im = jnp.broadcast_to(lb_im, bu_im.shape)
    _, _, xr, xi = lax.associative_scan(complex_affine_combine, (a_re, a_im, bu_re, bu_im), axis=1)
    y = jnp.einsum('bsgp,ghp->bsgh', xr, c_re.astype(f32)) - jnp.einsum('bsgp,ghp->bsgh', xi, c_im.astype(f32))
    y = y.reshape(B, S, W) + d_skip.astype(f32) * u.astype(f32)
    return y.astype(u.dtype), xr[:, -1], xi[:, -1]


def gated_delta_rule(q, k, v, g, beta, s0):
    B, S, H, dk = q.shape
    dv = v.shape[-1]
    C = math.gcd(S, GDN_CHUNK)
    n = S // C

    def c4(t):
        return t.reshape(B, n, C, H, t.shape[-1]).transpose(1, 0, 3, 2, 4)

    def c3(t):
        return t.reshape(B, n, C, H).transpose(1, 0, 3, 2)

    qc = c4(q * (dk ** -0.5))
    kc = c4(k)
    vc = c4(v)
    bc = c3(beta)
    gc = jnp.cumsum(c3(g), axis=-1)
    eye = jnp.eye(C, dtype=jnp.float32)
    lower = jnp.tril(jnp.ones((C, C), dtype=bool))
    strict = jnp.tril(jnp.ones((C, C), dtype=bool), -1)
    diff = gc[..., :, None] - gc[..., None, :]
    decay = jnp.where(lower, jnp.exp(jnp.where(lower, diff, 0.0)), 0.0)
    kbeta = kc * bc[..., None]
    vbeta = vc * bc[..., None]
    a = jnp.where(strict, jnp.einsum('nbhcd,nbhed->nbhce', kbeta, kc) * decay, 0.0) + eye
    t = lax.linalg.triangular_solve(a, jnp.broadcast_to(eye, a.shape), left_side=True, lower=True, unit_diagonal=True)
    u = t @ vbeta
    w = t @ (kbeta * jnp.exp(gc)[..., None])
    attn = jnp.einsum('nbhcd,nbhed->nbhce', qc, kc) * decay

    def step(state, xs):
        qi, ki, ui, wi, gi, ai = xs
        v_new = ui - wi @ state
        o = (qi * jnp.exp(gi)[..., None]) @ state + ai @ v_new
        g_last = gi[..., -1:]
        state = state * jnp.exp(g_last)[..., None] + jnp.einsum('bhcd,bhce->bhde', ki * jnp.exp(g_last - gi)[..., None], v_new)
        return state, o

    s_fin, o = lax.scan(step, s0, (qc, kc, u, w, gc, attn))
    return o.transpose(1, 0, 3, 2, 4).reshape(B, S, H, dv), s_fin


def trunk_layer(x, c, W, past):
    B, S, _ = x.shape
    f32 = jnp.float32
    mod = jnp.einsum('bd,de->be', jax.nn.silu(c), W['ada_w']) + W['ada_b']
    sh1, sc1, ga1, sh2, sc2, ga2 = jnp.split(mod[:, None, :], N_ADA, axis=-1)
    h = rms_norm(x, W['norm1']) * (1 + sc1) + sh1
    proj = jnp.einsum('bsd,de->bse', h, W['w_in'])
    pa, pb, pc, pd, pg = jnp.split(proj, IN_SPLITS, axis=-1)

    qa, ka, va = jnp.split(pa, 3, axis=-1)
    qa = rms_norm(split_heads(qa), W['moba_qn'])
    ka = rms_norm(split_heads(ka), W['moba_kn'])
    va = split_heads(va)
    if past is None:
        ka_all, va_all = ka, va
    else:
        ka_all = jnp.concatenate([past['moba_k'].astype(ka.dtype), ka], axis=1)
        va_all = jnp.concatenate([past['moba_v'].astype(va.dtype), va], axis=1)
    oa = moba_attention(qa, ka_all, va_all).reshape(B, S, BRANCH_WIDTH)

    qb, kb, vb, fb = jnp.split(pb, (BRANCH_WIDTH, 2 * BRANCH_WIDTH, 3 * BRANCH_WIDTH), axis=-1)
    qb = rms_norm(split_heads(qb), W['fox_qn'])
    kb = rms_norm(split_heads(kb), W['fox_kn'])
    vb = split_heads(vb)
    logf = jax.nn.log_sigmoid(fb.astype(f32) + W['fox_fb'].astype(f32))
    if past is None:
        kb_all, vb_all, logf_all = kb, vb, logf
    else:
        kb_all = jnp.concatenate([past['fox_k'].astype(kb.dtype), kb], axis=1)
        vb_all = jnp.concatenate([past['fox_v'].astype(vb.dtype), vb], axis=1)
        logf_all = jnp.concatenate([past['fox_logf'].astype(f32), logf], axis=1)
    cum_f = jnp.cumsum(logf_all, axis=1)
    ob = fox_attention(qb, kb_all, vb_all, cum_f[:, -S:], cum_f).reshape(B, S, BRANCH_WIDTH)

    if past is None:
        h0_re = jnp.zeros((B, S5_GROUPS, S5_STATE), f32)
        h0_im = jnp.zeros((B, S5_GROUPS, S5_STATE), f32)
    else:
        h0_re, h0_im = past['s5_re'], past['s5_im']
    yc, s_re, s_im = s5_ssm(pc, h0_re, h0_im, W['s5_lre'], W['s5_lim'], W['s5_bre'], W['s5_bim'],
                            W['s5_cre'], W['s5_cim'], W['s5_d'], W['s5_ldt'])
    yc = jax.nn.gelu(yc)
    oc = yc * jax.nn.sigmoid(jnp.einsum('bsw,wv->bsv', yc, W['s5_glu']))

    qkv_d, zd, beta_l, a_l = jnp.split(pd, (3 * BRANCH_WIDTH, 4 * BRANCH_WIDTH, 4 * BRANCH_WIDTH + H_GDN), axis=-1)
    if past is None:
        buf = jnp.zeros((B, GDN_CONV - 1, 3 * BRANCH_WIDTH), qkv_d.dtype)
        s0 = jnp.zeros((B, H_GDN, HEAD_DIM, HEAD_DIM), f32)
    else:
        buf = past['gdn_conv'].astype(qkv_d.dtype)
        s0 = past['gdn'].astype(f32)
    conv_in = jnp.concatenate([buf, qkv_d], axis=1)
    conv = conv_in[:, 0:S] * W['gdn_conv'][0]
    for j in range(1, GDN_CONV):
        conv = conv + conv_in[:, j:j + S] * W['gdn_conv'][j]
    conv = jax.nn.silu(conv)
    new_conv = conv_in[:, -(GDN_CONV - 1):]
    qd, kd, vd = jnp.split(conv, 3, axis=-1)
    qd = l2_norm(split_heads(qd))
    kd = l2_norm(split_heads(kd))
    vd = split_heads(vd).astype(f32)
    beta = jax.nn.sigmoid(beta_l.astype(f32))
    g = -jnp.exp(W['gdn_alog'].astype(f32)) * jax.nn.softplus(a_l.astype(f32) + W['gdn_dtb'].astype(f32))
    od, s_fin = gated_delta_rule(qd, kd, vd, g, beta, s0)
    od = rms_norm(od, W['gdn_on']) * jax.nn.silu(split_heads(zd).astype(f32))
    od = od.reshape(B, S, BRANCH_WIDTH).astype(x.dtype)

    branches = jnp.stack([oa, ob, oc.astype(x.dtype), od], axis=2)
    up = jnp.einsum('bsnw,nwd->bsnd', branches, W['w_branch'])
    gates = jax.nn.sigmoid(pg.reshape(B, S, N_BRANCH, D_MODEL))
    mixed = jnp.sum(gates * up, axis=2)
    x = x + ga1 * jnp.einsum('bsd,de->bse', mixed, W['w_out'])

    h2 = rms_norm(x, W['norm2']) * (1 + sc2) + sh2
    ff = jnp.square(jax.nn.relu(jnp.einsum('bsd,df->bsf', h2, W['mlp_in'])))
    x = x + ga2 * jnp.einsum('bsf,fd->bsd', ff, W['mlp_out'])

    new = {'moba_k': ka, 'moba_v': va, 'fox_k': kb, 'fox_v': vb, 'fox_logf': logf,
           's5_re': s_re, 's5_im': s_im, 'gdn': s_fin, 'gdn_conv': new_conv}
    return x, new


def setup_inputs(seed: int = 0) -> dict:
    key = jax.random.key(seed)
    ks = iter(jax.random.split(key, 64))

    def nrm(shape, scale):
        return scale * jax.random.normal(next(ks), shape, jnp.float32)

    n_pages = PAST_LEN // PAGE_SIZE
    n_used = DEC_BATCH * n_pages
    n_pool = n_used + max(1, n_used // 4)
    inp = {}
    inp['x_prompt'] = nrm((BATCH, SEQ, D_MODEL), 1.0)
    inp['x_sample'] = nrm((DEC_BATCH, DEC_SEQ, D_MODEL), 1.0)
    inp['cache_moba_k'] = nrm((DEPTH, n_pool, PAGE_SIZE, H_MOBA, HEAD_DIM), 1.0)
    inp['cache_moba_v'] = nrm((DEPTH, n_pool, PAGE_SIZE, H_MOBA, HEAD_DIM), 1.0)
    inp['cache_fox_k'] = nrm((DEPTH, n_pool, PAGE_SIZE, H_FOX, HEAD_DIM), 1.0)
    inp['cache_fox_v'] = nrm((DEPTH, n_pool, PAGE_SIZE, H_FOX, HEAD_DIM), 1.0)
    inp['cache_fox_logf'] = jax.nn.log_sigmoid(2.0 + nrm((DEPTH, n_pool, PAGE_SIZE, H_FOX), 1.0))
    inp['state_s5_re'] = nrm((DEPTH, DEC_BATCH, S5_GROUPS, S5_STATE), 0.5)
    inp['state_s5_im'] = nrm((DEPTH, DEC_BATCH, S5_GROUPS, S5_STATE), 0.5)
    inp['state_gdn'] = nrm((DEPTH, DEC_BATCH, H_GDN, HEAD_DIM, HEAD_DIM), 0.1)
    inp['state_gdn_conv'] = nrm((DEPTH, DEC_BATCH, GDN_CONV - 1, 3 * BRANCH_WIDTH), 1.0)
    inp['page_table'] = jax.random.permutation(next(ks), n_pool)[:n_used].reshape(DEC_BATCH, n_pages).astype(jnp.int32)
    inp['c_prompt'] = nrm((BATCH, D_MODEL), 1.0)
    inp['c_sample'] = nrm((DEC_BATCH, D_MODEL), 1.0)
    inp['norm1_g'] = 1.0 + nrm((DEPTH, D_MODEL), 0.02)
    inp['norm2_g'] = 1.0 + nrm((DEPTH, D_MODEL), 0.02)
    inp['ada_w'] = nrm((DEPTH, D_MODEL, N_ADA * D_MODEL), 0.5 * D_MODEL ** -0.5)
    inp['ada_b'] = nrm((DEPTH, N_ADA * D_MODEL), 0.02)
    inp['w_in'] = nrm((DEPTH, D_MODEL, IN_COLS), D_MODEL ** -0.5)
    inp['moba_qn_g'] = 1.0 + nrm((DEPTH, HEAD_DIM), 0.02)
    inp['moba_kn_g'] = 1.0 + nrm((DEPTH, HEAD_DIM), 0.02)
    inp['fox_qn_g'] = 1.0 + nrm((DEPTH, HEAD_DIM), 0.02)
    inp['fox_kn_g'] = 1.0 + nrm((DEPTH, HEAD_DIM), 0.02)
    inp['fox_f_bias'] = 2.0 + nrm((DEPTH, H_FOX), 0.1)
    inp['s5_lambda_re'] = -0.5 + nrm((DEPTH, S5_GROUPS, S5_STATE), 0.01)
    inp['s5_lambda_im'] = math.pi * jnp.arange(S5_STATE, dtype=jnp.float32) + nrm((DEPTH, S5_GROUPS, S5_STATE), 0.01)
    inp['s5_b_re'] = nrm((DEPTH, S5_GROUPS, S5_STATE, S5_GROUP), (2.0 * S5_GROUP) ** -0.5)
    inp['s5_b_im'] = nrm((DEPTH, S5_GROUPS, S5_STATE, S5_GROUP), (2.0 * S5_GROUP) ** -0.5)
    inp['s5_c_re'] = nrm((DEPTH, S5_GROUPS, S5_GROUP, S5_STATE), S5_STATE ** -0.5)
    inp['s5_c_im'] = nrm((DEPTH, S5_GROUPS, S5_GROUP, S5_STATE), S5_STATE ** -0.5)
    inp['s5_d'] = nrm((DEPTH, BRANCH_WIDTH), 0.5)
    inp['s5_log_dt'] = jax.random.uniform(next(ks), (DEPTH, S5_GROUPS), jnp.float32, math.log(1e-3), math.log(1e-1))
    inp['s5_w_glu'] = nrm((DEPTH, BRANCH_WIDTH, BRANCH_WIDTH), BRANCH_WIDTH ** -0.5)
    inp['gdn_conv_w'] = nrm((DEPTH, GDN_CONV, 3 * BRANCH_WIDTH), GDN_CONV ** -0.5)
    inp['gdn_a_log'] = jnp.log(jax.random.uniform(next(ks), (DEPTH, H_GDN), jnp.float32, 1.0, 16.0))
    dt = jnp.exp(jax.random.uniform(next(ks), (DEPTH, H_GDN), jnp.float32, math.log(1e-3), math.log(1e-1)))
    inp['gdn_dt_bias'] = dt + jnp.log(-jnp.expm1(-dt))
    inp['gdn_out_g'] = 1.0 + nrm((DEPTH, HEAD_DIM), 0.02)
    inp['w_branch'] = nrm((DEPTH, N_BRANCH, BRANCH_WIDTH, D_MODEL), BRANCH_WIDTH ** -0.5)
    inp['w_out'] = nrm((DEPTH, D_MODEL, D_MODEL), D_MODEL ** -0.5)
    inp['mlp_in'] = nrm((DEPTH, D_MODEL, D_FF), D_MODEL ** -0.5)
    inp['mlp_out'] = nrm((DEPTH, D_FF, D_MODEL), D_FF ** -0.5)
    return inp


def reference(x_prompt, x_sample, cache_moba_k, cache_moba_v, cache_fox_k, cache_fox_v, cache_fox_logf,
              state_s5_re, state_s5_im, state_gdn, state_gdn_conv, page_table, c_prompt, c_sample,
              norm1_g, norm2_g, ada_w, ada_b, w_in, moba_qn_g, moba_kn_g, fox_qn_g, fox_kn_g, fox_f_bias,
              s5_lambda_re, s5_lambda_im, s5_b_re, s5_b_im, s5_c_re, s5_c_im, s5_d, s5_log_dt, s5_w_glu,
              gdn_conv_w, gdn_a_log, gdn_dt_bias, gdn_out_g, w_branch, w_out, mlp_in, mlp_out):
    y_p = x_prompt
    y_s = x_sample
    new_p = {name: [] for name in STATE_KEYS}
    new_s = {name: [] for name in STATE_KEYS}
    for l in range(DEPTH):
        W = {'norm1': norm1_g[l], 'norm2': norm2_g[l], 'ada_w': ada_w[l], 'ada_b': ada_b[l], 'w_in': w_in[l],
             'moba_qn': moba_qn_g[l], 'moba_kn': moba_kn_g[l], 'fox_qn': fox_qn_g[l], 'fox_kn': fox_kn_g[l],
             'fox_fb': fox_f_bias[l], 's5_lre': s5_lambda_re[l], 's5_lim': s5_lambda_im[l],
             's5_bre': s5_b_re[l], 's5_bim': s5_b_im[l], 's5_cre': s5_c_re[l], 's5_cim': s5_c_im[l],
             's5_d': s5_d[l], 's5_ldt': s5_log_dt[l], 's5_glu': s5_w_glu[l], 'gdn_conv': gdn_conv_w[l],
             'gdn_alog': gdn_a_log[l], 'gdn_dtb': gdn_dt_bias[l], 'gdn_on': gdn_out_g[l],
             'w_branch': w_branch[l], 'w_out': w_out[l], 'mlp_in': mlp_in[l], 'mlp_out': mlp_out[l]}
        y_p, st_p = trunk_layer(y_p, c_prompt, W, None)
        past = {'moba_k': gather_pages(cache_moba_k, l, page_table),
                'moba_v': gather_pages(cache_moba_v, l, page_table),
                'fox_k': gather_pages(cache_fox_k, l, page_table),
                'fox_v': gather_pages(cache_fox_v, l, page_table),
                'fox_logf': gather_pages(cache_fox_logf, l, page_table),
                's5_re': state_s5_re[l], 's5_im': state_s5_im[l],
                'gdn': state_gdn[l], 'gdn_conv': state_gdn_conv[l]}
        y_s, st_s = trunk_layer(y_s, c_sample, W, past)
        for name in STATE_KEYS:
            new_p[name].append(st_p[name])
            new_s[name].append(st_s[name])
    pn = {name: jnp.stack(new_p[name]) for name in STATE_KEYS}
    sn = {name: jnp.stack(new_s[name]) for name in STATE_KEYS}
    return (y_p, y_s,
            pn['moba_k'], pn['moba_v'], pn['fox_k'], pn['fox_v'], pn['fox_logf'],
            pn['s5_re'], pn['s5_im'], pn['gdn'], pn['gdn_conv'],
            sn['moba_k'], sn['moba_v'], sn['fox_k'], sn['fox_v'], sn['fox_logf'],
            sn['s5_re'], sn['s5_im'], sn['gdn'], sn['gdn_conv'])
```

```python
import functools

import jax
import jax.numpy as jnp
from jax import lax
from jax.experimental import pallas as pl
from jax.experimental.pallas import tpu as pltpu

HEAD_DIM = 64
N_BRANCH = 4
N_HEADS = 4
BRANCH_WIDTH = N_HEADS * HEAD_DIM
MOBA_BLOCK = 256
MOBA_TOPK = 3
S5_GROUP = 16
S5_STATE = 64
GDN_CONV = 4
GDN_CHUNK = 64
N_ADA = 6
EPS = 1e-6
NEG_INF = -1e30
LANES = 128
SMALL_COLS = 128
VMEM_LIMIT_BYTES = 56 * 1024 * 1024

_F32 = jnp.float32
_BF16 = jnp.bfloat16
_HIGHEST = lax.Precision.HIGHEST
_NT = (((1,), (1,)), ((), ()))


def _params(*semantics):
    return pltpu.CompilerParams(dimension_semantics=semantics, vmem_limit_bytes=VMEM_LIMIT_BYTES)


def _dot(a, b):
    return jnp.dot(a.astype(_BF16), b.astype(_BF16), preferred_element_type=_F32)


def _dot_nt(a, b):
    return lax.dot_general(a.astype(_BF16), b.astype(_BF16), _NT, preferred_element_type=_F32)


def _split3(x):
    hi = x.astype(_BF16)
    r1 = x - hi.astype(_F32)
    mid = r1.astype(_BF16)
    lo = (r1 - mid.astype(_F32)).astype(_BF16)
    return hi, mid, lo


def _dot_x_exact(x, m):
    hi, mid, lo = _split3(x)
    d = lambda a: jnp.dot(a, m, preferred_element_type=_F32)
    return d(hi) + d(mid) + d(lo)


def _exact_dot_x(m, x):
    hi, mid, lo = _split3(x)
    d = lambda a: jnp.dot(m, a, preferred_element_type=_F32)
    return d(hi) + d(mid) + d(lo)


def _const_spec(shape):
    nd = len(shape)
    return pl.BlockSpec(shape, lambda *_: (0,) * nd)


def _ada_kernel(c_ref, w_ref, b_ref, o_ref):
    c = c_ref[...]
    o_ref[...] = _dot(c * jax.nn.sigmoid(c), w_ref[...]) + b_ref[...]


def _ada_call(c_all, ada_w, ada_b):
    depth, d, n = ada_w.shape
    rows = c_all.shape[0]
    tn = n // 4
    return pl.pallas_call(
        _ada_kernel,
        grid=(depth, n // tn),
        in_specs=[pl.BlockSpec((rows, d), lambda l, j: (0, 0)),
                  pl.BlockSpec((None, d, tn), lambda l, j: (l, 0, j)),
                  pl.BlockSpec((None, 1, tn), lambda l, j: (l, 0, j))],
        out_specs=pl.BlockSpec((None, rows, tn), lambda l, j: (l, 0, j)),
        out_shape=jax.ShapeDtypeStruct((depth, rows, n), _F32),
        compiler_params=_params("parallel", "parallel"),
        name="ada_mod",
    )(c_all, ada_w, ada_b.reshape(depth, 1, n))


_N_SLABS = 11 + 4 * N_BRANCH


def _scalar_gates(v_raw, bias, alog, index):
    v = v_raw + bias
    logf = jax.nn.log_sigmoid(v)
    beta = jax.nn.sigmoid(v_raw)
    g = -jnp.exp(alog) * jax.nn.softplus(v)
    nh = N_HEADS
    return jnp.where(index < nh, logf, jnp.where(index < 2 * nh, beta, jnp.where(index < 3 * nh, g, 0.0)))


def _mod_spec(mod, seq, tm):
    if seq > 1:
        tiles_per_batch = seq // tm
        return pl.BlockSpec((None, 1, mod.shape[-1]), lambda i: (i // tiles_per_batch, 0, 0))
    return pl.BlockSpec((tm, mod.shape[-1]), lambda i: (i, 0))


def _inproj_kernel(*refs, transpose_kv):
    if transpose_kv:
        (x_ref, mod_ref, g1_ref, wm_ref, ws_ref, gains_ref, sp_ref, seg_ref, wkvt_ref, wst_ref, gt_ref, spt_ref,
         qa_ref, ka_ref, va_ref, qb_ref, kb_ref, vb_ref, u_ref, qkvd_ref, zd_ref, gates_ref, sm_ref, smt_ref) = refs
    else:
        (x_ref, mod_ref, g1_ref, wm_ref, ws_ref, gains_ref, sp_ref, seg_ref,
         qa_ref, ka_ref, va_ref, qb_ref, kb_ref, vb_ref, u_ref, qkvd_ref, zd_ref, gates_ref, sm_ref) = refs
    d = x_ref.shape[1]
    x = x_ref[...]
    y = x * lax.rsqrt(jnp.mean(x * x, axis=-1, keepdims=True) + EPS) * g1_ref[...]
    h = y * (1.0 + mod_ref[:, d:2 * d]) + mod_ref[:, 0:d]
    hb = h.astype(_BF16)
    w = BRANCH_WIDTH

    def slab(s):
        return jnp.dot(hb, wm_ref[:, s * w:(s + 1) * w], preferred_element_type=_F32)

    def head_norm(p, row):
        ss = _dot_x_exact(p * p, seg_ref[...])
        return p * lax.rsqrt(ss * (1.0 / HEAD_DIM) + EPS) * gains_ref[row:row + 1, :]

    def slab_t(i):
        return lax.dot_general(wkvt_ref[i * w:(i + 1) * w, :], hb, _NT, preferred_element_type=_F32)

    def put_head_norm_t(ref, p, col):
        for hd in range(N_HEADS):
            hs = slice(hd * HEAD_DIM, (hd + 1) * HEAD_DIM)
            ph = p[hs, :]
            ms = jnp.mean(ph * ph, axis=0, keepdims=True)
            ref[hs, :] = ph * lax.rsqrt(ms + EPS) * gt_ref[hs, col:col + 1]

    qa_ref[...] = head_norm(slab(0), 0)
    qb_ref[...] = head_norm(slab(3), 2)
    if transpose_kv:
        put_head_norm_t(ka_ref, slab_t(0), 0)
        va_ref[...] = slab_t(1)
        put_head_norm_t(kb_ref, slab_t(2), 1)
        vb_ref[...] = slab_t(3)
    else:
        ka_ref[...] = head_norm(slab(1), 1)
        va_ref[...] = slab(2)
        kb_ref[...] = head_norm(slab(4), 3)
        vb_ref[...] = slab(5)
    u_ref[...] = slab(6)
    for i in range(3):
        qkvd_ref[:, i * w:(i + 1) * w] = slab(7 + i)
    zd_ref[...] = slab(10)
    for i in range(4 * N_BRANCH):
        gates_ref[:, i * w:(i + 1) * w] = slab(11 + i)

    sm = jnp.dot(hb, ws_ref[...], preferred_element_type=_F32)
    sm_ref[...] = _scalar_gates(sm, sp_ref[0:1, :], sp_ref[1:2, :], lax.broadcasted_iota(jnp.int32, sm.shape, 1))
    if transpose_kv:
        smt = lax.dot_general(wst_ref[...], hb, _NT, preferred_element_type=_F32)
        smt_ref[...] = _scalar_gates(smt, spt_ref[:, 0:1], spt_ref[:, 1:2], lax.broadcasted_iota(jnp.int32, smt.shape, 0))


def _inproj_call(x2d, mod, lw, seg, *, batch, seq, tm, transpose_kv):
    t, d = x2d.shape
    w = BRANCH_WIDTH
    tiles_per_batch = max(seq // tm, 1)
    row = lambda n: pl.BlockSpec((tm, n), lambda i: (i, 0))
    rs = lambda n: jax.ShapeDtypeStruct((t, n), _F32)
    args = [x2d, mod, lw["g1"], lw["wm"], lw["ws"], lw["gains"], lw["sp"], seg]
    if transpose_kv:
        args += [lw["wkvt"], lw["wst"], lw["gains_t"], lw["sp_t"]]
        kv_spec = pl.BlockSpec((None, w, tm), lambda i: (i // tiles_per_batch, 0, i % tiles_per_batch))
        kv_shape = jax.ShapeDtypeStruct((batch, w, seq), _F32)
    else:
        kv_spec = row(w)
        kv_shape = rs(w)
    out_specs = [row(w), kv_spec, kv_spec, row(w), kv_spec, kv_spec, row(w), row(3 * w), row(w), row(4 * N_BRANCH * w),
                 row(SMALL_COLS)]
    out_shape = [rs(w), kv_shape, kv_shape, rs(w), kv_shape, kv_shape, rs(w), rs(3 * w), rs(w), rs(4 * N_BRANCH * w),
                 rs(SMALL_COLS)]
    if transpose_kv:
        out_specs.append(pl.BlockSpec((None, 16, tm), lambda i: (i // tiles_per_batch, 0, i % tiles_per_batch)))
        out_shape.append(jax.ShapeDtypeStruct((batch, 16, seq), _F32))
    return pl.pallas_call(
        functools.partial(_inproj_kernel, transpose_kv=transpose_kv),
        grid=(t // tm,),
        in_specs=[row(d), _mod_spec(mod, seq, tm)] + [_const_spec(a.shape) for a in args[2:]],
        out_specs=out_specs,
        out_shape=out_shape,
        compiler_params=_params("parallel"),
        name="inproj",
    )(*args)


def _cumsum_rows_kernel(x_ref, tri_ref, o_ref):
    blk = tri_ref.shape[0]
    carry = jnp.zeros((x_ref.shape[0], 1), _F32)
    for j in range(x_ref.shape[1] // blk):
        c = _dot_x_exact(x_ref[:, j * blk:(j + 1) * blk], tri_ref[...]) + carry
        o_ref[:, j * blk:(j + 1) * blk] = c
        carry = c[:, blk - 1:blk]


def _cumsum_rows_call(xt, tri):
    b, r, s = xt.shape
    return pl.pallas_call(
        _cumsum_rows_kernel,
        grid=(b,),
        in_specs=[pl.BlockSpec((None, r, s), lambda i: (i, 0, 0)), _const_spec(tri.shape)],
        out_specs=pl.BlockSpec((None, r, s), lambda i: (i, 0, 0)),
        out_shape=jax.ShapeDtypeStruct((b, r, s), _F32),
        compiler_params=_params("parallel"),
        name="cumsum_rows",
    )(xt, tri)


def _alibi_slope(h):
    return float(2.0 ** (-8.0 * (h + 1) / N_HEADS))


def _online_update(carry, s, vt):
    m, l, acc = carry
    m_new = jnp.maximum(m, jnp.max(s, axis=1, keepdims=True))
    alpha = jnp.exp(m - m_new)
    p = jnp.exp(s - m_new)
    return m_new, alpha * l + jnp.sum(p, axis=1, keepdims=True), alpha * acc + _dot_nt(p, vt)


def _moba_prompt_kernel(q_ref, kt_ref, vt_ref, o_ref, kmean_ref, *, nb):
    qi = pl.program_id(1)
    tq = q_ref.shape[0]
    blk = MOBA_BLOCK

    @pl.when(qi == 0)
    def _():
        lane = lax.broadcasted_iota(jnp.int32, kmean_ref.shape, 1)
        km = jnp.zeros(kmean_ref.shape, _F32)
        for j in range(nb):
            col = jnp.mean(kt_ref[:, j * blk:(j + 1) * blk], axis=1, keepdims=True)
            km = jnp.where(lane == j, col, km)
        kmean_ref[...] = km

    lane_q = lax.broadcasted_iota(jnp.int32, (tq, LANES), 1)
    rc = (lax.broadcasted_iota(jnp.int32, (tq, blk), 0) - lax.broadcasted_iota(jnp.int32, (tq, blk), 1)).astype(_F32)
    own_start = pl.multiple_of(qi * blk, blk)
    for h in range(N_HEADS):
        hs = slice(h * HEAD_DIM, (h + 1) * HEAD_DIM)
        qh = q_ref[:, hs]
        gate = jnp.dot(qh, kmean_ref[hs, :], precision=_HIGHEST, preferred_element_type=_F32)
        valid = lane_q < qi
        gv = jnp.where(valid, gate, NEG_INF)
        cnt = jnp.zeros((tq, LANES), _F32)
        for jp in range(nb):
            col = gv[:, jp:jp + 1]
            cnt = cnt + ((col > gv) | ((col == gv) & (jp < lane_q))).astype(_F32)
        sel = (valid & (cnt < MOBA_TOPK)).astype(_F32)
        qs = (qh * (HEAD_DIM ** -0.5)).astype(_BF16)
        slope = _alibi_slope(h)

        def scores(start, blocks_back):
            kt = kt_ref[hs, pl.ds(start, blk)]
            s = jnp.dot(qs, kt.astype(_BF16), preferred_element_type=_F32)
            return s - slope * (rc + (blocks_back * blk).astype(_F32))

        s = jnp.where(rc >= 0, scores(own_start, qi - qi), NEG_INF)
        m = jnp.max(s, axis=1, keepdims=True)
        p = jnp.exp(s - m)
        carry = (m, jnp.sum(p, axis=1, keepdims=True), _dot_nt(p, vt_ref[hs, pl.ds(own_start, blk)]))

        def body(j, carry):
            start = pl.multiple_of(j * blk, blk)
            picked = jnp.sum(jnp.where(lane_q == j, sel, 0.0), axis=1, keepdims=True) > 0.5
            s = jnp.where(picked, scores(start, qi - j), NEG_INF)
            return _online_update(carry, s, vt_ref[hs, pl.ds(start, blk)])

        m, l, acc = lax.fori_loop(0, qi, body, carry)
        o_ref[:, hs] = acc / l


def _moba_prompt_call(q, kt, vt):
    b, s, w = q.shape
    assert s % MOBA_BLOCK == 0
    nb = s // MOBA_BLOCK
    assert nb <= LANES
    return pl.pallas_call(
        functools.partial(_moba_prompt_kernel, nb=nb),
        grid=(b, nb),
        in_specs=[pl.BlockSpec((None, MOBA_BLOCK, w), lambda i, j: (i, j, 0)),
                  pl.BlockSpec((None, w, s), lambda i, j: (i, 0, 0)),
                  pl.BlockSpec((None, w, s), lambda i, j: (i, 0, 0))],
        out_specs=pl.BlockSpec((None, MOBA_BLOCK, w), lambda i, j: (i, j, 0)),
        out_shape=jax.ShapeDtypeStruct((b, s, w), _F32),
        scratch_shapes=[pltpu.VMEM((w, LANES), _F32)],
        compiler_params=_params("parallel", "arbitrary"),
        name="moba_prompt",
    )(q, kt, vt)


def _fox_prompt_kernel(q_ref, kt_ref, vt_ref, cum_ref, o_ref):
    qi = pl.program_id(1)
    tq = q_ref.shape[0]
    blk = tq
    causal = lax.broadcasted_iota(jnp.int32, (tq, blk), 0) >= lax.broadcasted_iota(jnp.int32, (tq, blk), 1)
    own_start = pl.multiple_of(qi * blk, blk)
    for h in range(N_HEADS):
        hs = slice(h * HEAD_DIM, (h + 1) * HEAD_DIM)
        qs = (q_ref[:, hs] * (HEAD_DIM ** -0.5)).astype(_BF16)
        ck_own = cum_ref[h:h + 1, pl.ds(own_start, blk)]
        cref = ck_own[:, 0:1]

        def scores(start, ck):
            kt = kt_ref[hs, pl.ds(start, blk)]
            return jnp.dot(qs, kt.astype(_BF16), preferred_element_type=_F32) + (cref - ck)

        s = jnp.where(causal, scores(own_start, ck_own), NEG_INF)
        m = jnp.max(s, axis=1, keepdims=True)
        p = jnp.exp(s - m)
        carry = (m, jnp.sum(p, axis=1, keepdims=True), _dot_nt(p, vt_ref[hs, pl.ds(own_start, blk)]))

        def body(j, carry):
            start = pl.multiple_of(j * blk, blk)
            s = scores(start, cum_ref[h:h + 1, pl.ds(start, blk)])
            return _online_update(carry, s, vt_ref[hs, pl.ds(start, blk)])

        m, l, acc = lax.fori_loop(0, qi, body, carry)
        o_ref[:, hs] = acc / l


def _fox_prompt_call(q, kt, vt, cum_rows, *, tq):
    b, s, w = q.shape
    assert s % tq == 0
    r = cum_rows.shape[1]
    return pl.pallas_call(
        _fox_prompt_kernel,
        grid=(b, s // tq),
        in_specs=[pl.BlockSpec((None, tq, w), lambda i, j: (i, j, 0)),
                  pl.BlockSpec((None, w, s), lambda i, j: (i, 0, 0)),
                  pl.BlockSpec((None, w, s), lambda i, j: (i, 0, 0)),
                  pl.BlockSpec((None, r, s), lambda i, j: (i, 0, 0))],
        out_specs=pl.BlockSpec((None, tq, w), lambda i, j: (i, j, 0)),
        out_shape=jax.ShapeDtypeStruct((b, s, w), _F32),
        compiler_params=_params("parallel", "parallel"),
        name="fox_prompt",
    )(q, kt, vt, cum_rows)


def _s5_disc_kernel(lre_ref, lim_ref, ldt_ref, lbre_ref, lbim_ref, fre_ref, fim_ref):
    lre = lre_ref[...]
    lim = lim_ref[...]
    dt = jnp.exp(ldt_ref[...])
    mag = jnp.exp(lre * dt)
    lb_re = mag * jnp.cos(lim * dt)
    lb_im = mag * jnp.sin(lim * dt)
    den = lre * lre + lim * lim
    nr = lb_re - 1.0
    lbre_ref[...] = lb_re
    lbim_ref[...] = lb_im
    fre_ref[...] = (nr * lre + lb_im * lim) / den
    fim_ref[...] = (lb_im * lre - nr * lim) / den


def _s5_bbar_kernel(fre_ref, fim_ref, bre_ref, bim_ref, ore_ref, oim_ref):
    fre, fim, bre, bim = fre_ref[...], fim_ref[...], bre_ref[...], bim_ref[...]
    ore_ref[...] = fre * bre - fim * bim
    oim_ref[...] = fre * bim + fim * bre


def _whole_call(kernel, n_out, shape, name, *args):
    return pl.pallas_call(
        kernel,
        out_shape=[jax.ShapeDtypeStruct(shape, _F32)] * n_out,
        name=name,
    )(*args)


def _s5_kernel(u_ref, h0re_ref, h0im_ref, lbre_ref, lbim_ref, bbd_ref, cre_ref, cim_ref, d_ref, glu_ref,
               oc_ref, sre_ref, sim_ref, bu_ref, st_ref, *, steps, bp):
    n = lbre_ref.shape[1]

    @pl.when(pl.program_id(0) == 0)
    def _():
        st_ref[0] = h0re_ref[...]
        st_ref[1] = h0im_ref[...]

    u = u_ref[...]
    bu_ref[...] = _dot(u, bbd_ref[...])
    a_re = jnp.broadcast_to(lbre_ref[...], (bp, n))
    a_im = jnp.broadcast_to(lbim_ref[...], (bp, n))

    def step(t, carry):
        xr, xi = carry
        rows = pl.ds(pl.multiple_of(t * bp, bp), bp)
        nr = a_re * xr - a_im * xi + bu_ref[rows, 0:n]
        ni = a_re * xi + a_im * xr + bu_ref[rows, n:2 * n]
        bu_ref[rows, 0:n] = nr
        bu_ref[rows, n:2 * n] = ni
        return nr, ni

    xr, xi = lax.fori_loop(0, steps, step, (st_ref[0], st_ref[1]))
    st_ref[0] = xr
    st_ref[1] = xi
    sre_ref[...] = xr
    sim_ref[...] = xi
    y = _dot(bu_ref[:, 0:n], cre_ref[...]) - _dot(bu_ref[:, n:2 * n], cim_ref[...]) + d_ref[...] * u
    yc = jax.nn.gelu(y)
    oc_ref[...] = yc * jax.nn.sigmoid(_dot(yc, glu_ref[...]))


def _s5_call(u_tm, h0re, h0im, lbre, lbim, bbd, cre, cim, dskip, glu, *, steps):
    rows, w = u_tm.shape
    bp, n = h0re.shape
    total_steps = rows // bp
    assert total_steps % steps == 0 and bp % 8 == 0
    r = steps * bp
    st = jax.ShapeDtypeStruct((bp, n), _F32)
    return pl.pallas_call(
        functools.partial(_s5_kernel, steps=steps, bp=bp),
        grid=(total_steps // steps,),
        in_specs=[pl.BlockSpec((r, w), lambda c: (c, 0)), _const_spec((bp, n)), _const_spec((bp, n)),
                  _const_spec(lbre.shape), _const_spec(lbim.shape), _const_spec(bbd.shape), _const_spec(cre.shape),
                  _const_spec(cim.shape), _const_spec(dskip.shape), _const_spec(glu.shape)],
        out_specs=[pl.BlockSpec((r, w), lambda c: (c, 0)), _const_spec((bp, n)), _const_spec((bp, n))],
        out_shape=[jax.ShapeDtypeStruct((rows, w), _F32), st, st],
        scratch_shapes=[pltpu.VMEM((r, 2 * n), _F32), pltpu.VMEM((2, bp, n), _F32)],
        compiler_params=_params("arbitrary"),
        name="s5_scan",
    )(u_tm, h0re, h0im, lbre, lbim, bbd, cre, cim, dskip, glu)


def _unit_lower_inverse(nmat, eye):
    hdot = lambda a, b: jnp.dot(a, b, precision=_HIGHEST, preferred_element_type=_F32)
    size = nmat.shape[0]
    t = eye - nmat
    power = nmat
    span = 2
    while span < size:
        power = hdot(power, power)
        t = t + hdot(t, power)
        span *= 2
    return t


def _gdn_prompt_kernel(x_ref, z_ref, sm_ref, cw_ref, on_ref, seg_ref, tri_ref,
                       o_ref, s_out_ref, ext_ref, state_ref):
    c = GDN_CHUNK
    w = BRANCH_WIDTH
    n = pl.program_id(1)

    @pl.when(n == 0)
    def _():
        ext_ref[0:8, :] = jnp.zeros((8, 3 * w), _F32)
        state_ref[...] = jnp.zeros(state_ref.shape, _F32)

    ext_ref[8:8 + c, :] = x_ref[...]
    conv = cw_ref[0:1, :] * ext_ref[5:5 + c, :]
    for j in range(1, GDN_CONV):
        conv = conv + cw_ref[j:j + 1, :] * ext_ref[5 + j:5 + j + c, :]
    ext_ref[0:8, :] = ext_ref[c:c + 8, :]
    act = conv * jax.nn.sigmoid(conv)
    q, k, v = act[:, 0:w], act[:, w:2 * w], act[:, 2 * w:3 * w]
    l2 = lambda a: a * lax.rsqrt(_dot_x_exact(a * a, seg_ref[...]) + EPS)
    q = l2(q) * (HEAD_DIM ** -0.5)
    k = l2(k)
    sm = sm_ref[...]
    gc_all = _exact_dot_x(tri_ref[...], sm)
    row = lax.broadcasted_iota(jnp.int32, (c, c), 0)
    col = lax.broadcasted_iota(jnp.int32, (c, c), 1)
    eye = (row == col).astype(_F32)
    z = z_ref[...]
    for h in range(N_HEADS):
        hs = slice(h * HEAD_DIM, (h + 1) * HEAD_DIM)
        qh, kh, vh = q[:, hs], k[:, hs], v[:, hs]
        beta = sm[:, N_HEADS + h:N_HEADS + h + 1]
        gcol = gc_all[:, 2 * N_HEADS + h:2 * N_HEADS + h + 1]
        grow = jnp.sum(eye * gcol, axis=0, keepdims=True)
        lower = row >= col
        decay = jnp.where(lower, jnp.exp(jnp.where(lower, gcol - grow, 0.0)), 0.0)
        kb = kh * beta
        nmat = jnp.where(row > col, _dot_nt(kb, kh) * decay, 0.0)
        t = _unit_lower_inverse(nmat, eye)
        u = _dot(t, vh * beta)
        wmat = _dot(t, kb * jnp.exp(gcol))
        attn = _dot_nt(qh, kh) * decay
        state = state_ref[h]
        v_new = u - _dot(wmat, state)
        o = _dot(qh * jnp.exp(gcol), state) + _dot(attn, v_new)
        g_last = gcol[c - 1:c, :]
        kdec = kh * jnp.exp(g_last - gcol)
        state = state * jnp.exp(g_last) + _dot(_dot_nt(eye, kdec), v_new)
        state_ref[h] = state
        on = o * lax.rsqrt(jnp.mean(o * o, axis=-1, keepdims=True) + EPS) * on_ref[...]
        zh = z[:, hs]
        o_ref[:, hs] = on * (zh * jax.nn.sigmoid(zh))
    s_out_ref[...] = state_ref[...]


def _gdn_prompt_call(qkvd, zd, sm_rows, conv_w, on_gain, seg, tri):
    b, s, w3 = qkvd.shape
    w = w3 // 3
    c = GDN_CHUNK
    assert s % c == 0
    blk = lambda n: pl.BlockSpec((None, c, n), lambda i, j: (i, j, 0))
    return pl.pallas_call(
        _gdn_prompt_kernel,
        grid=(b, s // c),
        in_specs=[blk(w3), blk(w), blk(SMALL_COLS), _const_spec(conv_w.shape), _const_spec(on_gain.shape),
                  _const_spec(seg.shape), _const_spec(tri.shape)],
        out_specs=[blk(w), pl.BlockSpec((None, N_HEADS, HEAD_DIM, HEAD_DIM), lambda i, j: (i, 0, 0, 0))],
        out_shape=[jax.ShapeDtypeStruct((b, s, w), _F32),
                   jax.ShapeDtypeStruct((b, N_HEADS, HEAD_DIM, HEAD_DIM), _F32)],
        scratch_shapes=[pltpu.VMEM((c + 8, w3), _F32), pltpu.VMEM((N_HEADS, HEAD_DIM, HEAD_DIM), _F32)],
        compiler_params=_params("parallel", "arbitrary"),
        name="gdn_prompt",
    )(qkvd, zd, sm_rows, conv_w, on_gain, seg, tri)


def _merge_kernel(oa_ref, ob_ref, oc_ref, od_ref, gates_ref, x_ref, mod_ref, wb_ref, wo_ref, y_ref):
    d = x_ref.shape[1]
    mixed = None
    for i, ref in enumerate((oa_ref, ob_ref, oc_ref, od_ref)):
        term = jax.nn.sigmoid(gates_ref[:, i * d:(i + 1) * d]) * _dot(ref[...], wb_ref[i])
        mixed = term if mixed is None else mixed + term
    y_ref[...] = x_ref[...] + mod_ref[:, 2 * d:3 * d] * _dot(mixed, wo_ref[...])


def _mlp_kernel(x_ref, mod_ref, g2_ref, wi_ref, wo_ref, y_ref):
    d = x_ref.shape[1]
    x = x_ref[...]
    y = x * lax.rsqrt(jnp.mean(x * x, axis=-1, keepdims=True) + EPS) * g2_ref[...]
    h = y * (1.0 + mod_ref[:, 4 * d:5 * d]) + mod_ref[:, 3 * d:4 * d]
    ff = jnp.maximum(_dot(h, wi_ref[...]), 0.0)
    y_ref[...] = x + mod_ref[:, 5 * d:6 * d] * _dot(ff * ff, wo_ref[...])


def _merge_call(oa, ob, oc, od, gates, x2d, mod, wb, wo, *, seq, tm):
    t, d = x2d.shape
    row = lambda n: pl.BlockSpec((tm, n), lambda i: (i, 0))
    return pl.pallas_call(
        _merge_kernel,
        grid=(t // tm,),
        in_specs=[row(BRANCH_WIDTH)] * 4 + [row(gates.shape[1]), row(d), _mod_spec(mod, seq, tm),
                                            _const_spec(wb.shape), _const_spec(wo.shape)],
        out_specs=row(d),
        out_shape=jax.ShapeDtypeStruct((t, d), _F32),
        compiler_params=_params("parallel"),
        name="merge",
    )(oa, ob, oc, od, gates, x2d, mod, wb, wo)


def _mlp_call(x2d, mod, g2, wi, wo, *, seq, tm):
    t, d = x2d.shape
    row = lambda n: pl.BlockSpec((tm, n), lambda i: (i, 0))
    return pl.pallas_call(
        _mlp_kernel,
        grid=(t // tm,),
        in_specs=[row(d), _mod_spec(mod, seq, tm), _const_spec(g2.shape), _const_spec(wi.shape), _const_spec(wo.shape)],
        out_specs=row(d),
        out_shape=jax.ShapeDtypeStruct((t, d), _F32),
        compiler_params=_params("parallel"),
        name="mlp",
    )(x2d, mod, g2, wi, wo)


PAGES_PER_STEP = 8


def _head_rows(row_vec, rows=8):
    shape = (rows, row_vec.shape[1])
    head_of_lane = lax.shift_right_logical(lax.broadcasted_iota(jnp.int32, shape, 1), 6)
    return jnp.where(head_of_lane == lax.broadcasted_iota(jnp.int32, shape, 0), row_vec, 0.0)


def _page_spec(layer, rows, slot_fn):
    return pl.BlockSpec((None, None, rows, LANES), lambda *a: (layer,) + slot_fn(*a))


def _moba_select_kernel(pt_ref, q_ref, *refs, n_blocks):
    pages = refs[:PAGES_PER_STEP]
    ones_ref, o_ref, km_ref = refs[PAGES_PER_STEP:]
    g = pl.program_id(1)
    blocks_per_step = PAGES_PER_STEP // 2

    @pl.when(g == 0)
    def _():
        km_ref[...] = jnp.zeros(km_ref.shape, _F32)

    lane = lax.broadcasted_iota(jnp.int32, km_ref.shape, 1)
    km = km_ref[...]
    for i in range(blocks_per_step):
        both = pages[2 * i][...] + pages[2 * i + 1][...]
        sums = _dot_x_exact(both, ones_ref[...])
        km = jnp.where(lane == g * blocks_per_step + i, sums * (1.0 / MOBA_BLOCK), km)
    km_ref[...] = km

    @pl.when(g == pl.num_programs(1) - 1)
    def _():
        gate = jnp.dot(_head_rows(q_ref[...]), km, precision=_HIGHEST, preferred_element_type=_F32)
        lane8 = lax.broadcasted_iota(jnp.int32, gate.shape, 1)
        valid = lane8 < n_blocks
        gv = jnp.where(valid, gate, NEG_INF)
        cnt = jnp.zeros(gate.shape, _F32)
        for jp in range(n_blocks):
            col = gv[:, jp:jp + 1]
            cnt = cnt + ((col > gv) | ((col == gv) & (jp < lane8))).astype(_F32)
        sel = (valid & (cnt < MOBA_TOPK)).astype(_F32)
        pos = jnp.zeros(gate.shape, _F32)
        for jp in range(n_blocks):
            pos = pos + jnp.where(lane8 > jp, sel[:, jp:jp + 1], 0.0)
        out = jnp.zeros(gate.shape, jnp.int32)
        for slot in range(MOBA_TOPK):
            hit = (sel > 0.5) & (pos == float(slot))
            idx = jnp.sum(jnp.where(hit, lane8.astype(_F32), 0.0), axis=1, keepdims=True)
            out = jnp.where(lane8 == slot, idx.astype(jnp.int32), out)
        o_ref[...] = out


def _moba_select_call(layer, q3, cache_kt, page_table, ones):
    bs = q3.shape[0]
    n_pages = page_table.shape[1]
    assert n_pages % PAGES_PER_STEP == 0
    n_blocks = n_pages // 2
    assert MOBA_TOPK <= n_blocks <= LANES
    w = q3.shape[-1]
    page = lambda i: _page_spec(layer, w, lambda b, g, pt: (pt[b, g * PAGES_PER_STEP + i], 0, 0))
    return pl.pallas_call(
        functools.partial(_moba_select_kernel, n_blocks=n_blocks),
        grid_spec=pltpu.PrefetchScalarGridSpec(
            num_scalar_prefetch=1,
            grid=(bs, n_pages // PAGES_PER_STEP),
            in_specs=[pl.BlockSpec((None, 1, w), lambda b, g, pt: (b, 0, 0))]
            + [page(i) for i in range(PAGES_PER_STEP)] + [pl.BlockSpec(ones.shape, lambda b, g, pt: (0, 0))],
            out_specs=pl.BlockSpec((None, 8, LANES), lambda b, g, pt: (b, 0, 0)),
            scratch_shapes=[pltpu.VMEM((w, LANES), _F32)]),
        out_shape=jax.ShapeDtypeStruct((bs, 8, LANES), jnp.int32),
        compiler_params=_params("parallel", "arbitrary"),
        name="moba_select",
    )(page_table, q3, *([cache_kt] * PAGES_PER_STEP), ones)


def _moba_decode_kernel(pt_ref, sel_ref, q_ref, kn_ref, vn_ref, slope_ref, *refs, past_len):
    n_pg = 2 * MOBA_TOPK
    k_pages, v_pages, o_ref = refs[:n_pg], refs[n_pg:2 * n_pg], refs[2 * n_pg]
    b, h = pl.program_id(0), pl.program_id(1)
    qs = q_ref[...] * (HEAD_DIM ** -0.5)
    qs8 = jnp.broadcast_to(qs, (8, HEAD_DIM)).astype(_BF16)
    slope = slope_ref[...]
    lane = lax.broadcasted_iota(jnp.int32, (1, LANES), 1)
    s_own = jnp.sum(qs * kn_ref[...], axis=1, keepdims=True)
    scores = []
    for i in range(n_pg):
        blk = sel_ref[b, h * MOBA_TOPK + i // 2]
        pos = blk * MOBA_BLOCK + (i % 2) * LANES + lane
        s = jnp.dot(qs8, k_pages[i][...].astype(_BF16), preferred_element_type=_F32)[0:1, :]
        scores.append(s - slope * (past_len - pos).astype(_F32))
    m = s_own
    for s in scores:
        m = jnp.maximum(m, jnp.max(s, axis=1, keepdims=True))
    p_own = jnp.exp(s_own - m)
    l = p_own
    acc = p_own * vn_ref[...]
    for s, v_ref in zip(scores, v_pages):
        p = jnp.exp(s - m)
        l = l + jnp.sum(p, axis=1, keepdims=True)
        acc = acc + _dot_nt(jnp.broadcast_to(p, (8, LANES)), v_ref[...])[0:1, :]
    o_ref[...] = acc / l


def _moba_decode_call(layer, q4, kn4, vn4, sel, cache_kt, cache_vt, page_table, slopes, past_len):
    bs, nh = q4.shape[:2]
    vec = pl.BlockSpec((None, None, 1, HEAD_DIM), lambda b, h, pt, sl: (b, h, 0, 0))

    def page(i):
        return _page_spec(layer, HEAD_DIM,
                          lambda b, h, pt, sl: (pt[b, 2 * sl[b, h * MOBA_TOPK + i // 2] + i % 2], h, 0))

    pages = [page(i) for i in range(2 * MOBA_TOPK)]
    return pl.pallas_call(
        functools.partial(_moba_decode_kernel, past_len=past_len),
        grid_spec=pltpu.PrefetchScalarGridSpec(
            num_scalar_prefetch=2,
            grid=(bs, nh),
            in_specs=[vec, vec, vec, pl.BlockSpec((None, 1, LANES), lambda b, h, pt, sl: (h, 0, 0))] + pages + pages,
            out_specs=vec),
        out_shape=jax.ShapeDtypeStruct(q4.shape, _F32),
        compiler_params=_params("parallel", "parallel"),
        name="moba_decode",
    )(page_table, sel, q4, kn4, vn4, slopes, *([cache_kt] * len(pages)), *([cache_vt] * len(pages)))


def _fox_decode_kernel(pt_ref, q_ref, kn_ref, vn_ref, smn_ref, *refs):
    n = PAGES_PER_STEP
    k_pages, v_pages, f_pages = refs[:n], refs[n:2 * n], refs[2 * n:3 * n]
    suf_ref, o_ref, m_ref, l_ref, acc_ref, carry_ref = refs[3 * n:]
    gi = pl.program_id(1)
    qbd = _head_rows(q_ref[...] * (HEAD_DIM ** -0.5))
    row = lax.broadcasted_iota(jnp.int32, (8, LANES), 0)
    lane = lax.broadcasted_iota(jnp.int32, (8, LANES), 1)

    @pl.when(gi == 0)
    def _():
        m_ref[...] = jnp.sum(qbd * kn_ref[...], axis=1, keepdims=True)
        l_ref[...] = jnp.ones(l_ref.shape, _F32)
        acc_ref[...] = jnp.broadcast_to(vn_ref[...], acc_ref.shape)
        carry_ref[...] = jnp.sum(jnp.where((lane == row) & (row < N_HEADS), smn_ref[...], 0.0), axis=1, keepdims=True)

    carry = (m_ref[...], l_ref[...], acc_ref[...])
    suffix = carry_ref[...]
    qb = qbd.astype(_BF16)
    for i in reversed(range(n)):
        lf = f_pages[i][...]
        r = _dot_x_exact(lf, suf_ref[...]) + suffix
        suffix = r[:, 0:1] + lf[:, 0:1]
        s = jnp.dot(qb, k_pages[i][...].astype(_BF16), preferred_element_type=_F32) + r
        carry = _online_update(carry, s, v_pages[i][...])
    m_ref[...], l_ref[...], acc_ref[...] = carry
    carry_ref[...] = suffix

    @pl.when(gi == pl.num_programs(1) - 1)
    def _():
        o_ref[...] = jnp.sum(_head_rows(carry[2] / carry[1]), axis=0, keepdims=True)


def _fox_decode_call(layer, q3, kn3, vn3, smn3, cache_kt, cache_vt, cache_lf, page_table, suf):
    bs, _, w = q3.shape
    n_pages = page_table.shape[1]
    n_groups = n_pages // PAGES_PER_STEP
    assert n_pages % PAGES_PER_STEP == 0

    def page(i, rows):
        return _page_spec(layer, rows, lambda b, g, pt: (pt[b, (n_groups - 1 - g) * PAGES_PER_STEP + i], 0, 0))

    vec = lambda n: pl.BlockSpec((None, 1, n), lambda b, g, pt: (b, 0, 0))
    rng = range(PAGES_PER_STEP)
    return pl.pallas_call(
        _fox_decode_kernel,
        grid_spec=pltpu.PrefetchScalarGridSpec(
            num_scalar_prefetch=1,
            grid=(bs, n_groups),
            in_specs=[vec(w), vec(w), vec(w), vec(SMALL_COLS)] + [page(i, w) for i in rng] + [page(i, w) for i in rng]
            + [page(i, 8) for i in rng] + [pl.BlockSpec(suf.shape, lambda b, g, pt: (0, 0))],
            out_specs=vec(w),
            scratch_shapes=[pltpu.VMEM((8, 1), _F32), pltpu.VMEM((8, 1), _F32), pltpu.VMEM((8, w), _F32),
                            pltpu.VMEM((8, 1), _F32)]),
        out_shape=jax.ShapeDtypeStruct((bs, 1, w), _F32),
        compiler_params=_params("parallel", "arbitrary"),
        name="fox_decode",
    )(page_table, q3, kn3, vn3, smn3, *([cache_kt] * PAGES_PER_STEP), *([cache_vt] * PAGES_PER_STEP),
      *([cache_lf] * PAGES_PER_STEP), suf)


def _gdn_decode_kernel(x_ref, buf_ref, z_ref, sm_ref, cw_ref, on_ref, s0_ref, o_ref, s_out_ref):
    w = BRANCH_WIDTH
    conv = cw_ref[GDN_CONV - 1:GDN_CONV, :] * x_ref[...]
    for j in range(GDN_CONV - 1):
        conv = conv + cw_ref[j:j + 1, :] * buf_ref[j:j + 1, :]
    act = conv * jax.nn.sigmoid(conv)
    sm = sm_ref[...]
    z = z_ref[...]
    eye = lax.broadcasted_iota(jnp.int32, (HEAD_DIM, HEAD_DIM), 0) == lax.broadcasted_iota(jnp.int32, (HEAD_DIM, HEAD_DIM), 1)
    to_col = lambda r: jnp.sum(jnp.where(eye, r, 0.0), axis=1, keepdims=True)
    l2 = lambda a: a * lax.rsqrt(jnp.sum(a * a, axis=1, keepdims=True) + EPS)
    for h in range(N_HEADS):
        hs = slice(h * HEAD_DIM, (h + 1) * HEAD_DIM)
        qh = l2(act[:, hs]) * (HEAD_DIM ** -0.5)
        kh = l2(act[:, w + h * HEAD_DIM:w + (h + 1) * HEAD_DIM])
        vh = act[:, 2 * w + h * HEAD_DIM:2 * w + (h + 1) * HEAD_DIM]
        beta = sm[:, N_HEADS + h:N_HEADS + h + 1]
        eg = jnp.exp(sm[:, 2 * N_HEADS + h:2 * N_HEADS + h + 1])
        state = s0_ref[h]
        kcol, qcol = to_col(kh), to_col(qh)
        v_new = beta * (vh - eg * jnp.sum(kcol * state, axis=0, keepdims=True))
        o = eg * jnp.sum(qcol * state, axis=0, keepdims=True) + jnp.sum(qh * kh, axis=1, keepdims=True) * v_new
        s_out_ref[h] = state * eg + kcol * v_new
        on = o * lax.rsqrt(jnp.mean(o * o, axis=-1, keepdims=True) + EPS) * on_ref[...]
        zh = z[:, hs]
        o_ref[:, hs] = on * (zh * jax.nn.sigmoid(zh))


def _gdn_decode_call(x3, buf, z3, sm3, conv_w, on_gain, s0):
    bs = x3.shape[0]
    w = BRANCH_WIDTH
    vec = lambda n: pl.BlockSpec((None, 1, n), lambda b: (b, 0, 0))
    st = pl.BlockSpec((None, N_HEADS, HEAD_DIM, HEAD_DIM), lambda b: (b, 0, 0, 0))
    return pl.pallas_call(
        _gdn_decode_kernel,
        grid=(bs,),
        in_specs=[vec(3 * w), pl.BlockSpec((None, GDN_CONV - 1, 3 * w), lambda b: (b, 0, 0)), vec(w), vec(SMALL_COLS),
                  _const_spec(conv_w.shape), _const_spec(on_gain.shape), st],
        out_specs=[vec(w), st],
        out_shape=[jax.ShapeDtypeStruct((bs, 1, w), _F32), jax.ShapeDtypeStruct(s0.shape, _F32)],
        compiler_params=_params("parallel"),
        name="gdn_decode",
    )(x3, buf, z3, sm3, conv_w, on_gain, s0)


def _constants():
    w = BRANCH_WIDTH
    i = jnp.arange(w)
    c = jnp.arange(GDN_CHUNK)
    return dict(
        seg=(i[:, None] // HEAD_DIM == i[None, :] // HEAD_DIM).astype(_BF16),
        tri_u=(i[:, None] <= i[None, :]).astype(_BF16),
        tri_l=(c[:, None] >= c[None, :]).astype(_BF16),
        ones=jnp.ones((LANES, LANES), _BF16),
        suf=(jnp.arange(LANES)[:, None] > jnp.arange(LANES)[None, :]).astype(_BF16),
        slopes=jnp.broadcast_to(jnp.asarray([_alibi_slope(h) for h in range(N_HEADS)], _F32)[:, None, None],
                                (N_HEADS, 1, LANES)),
    )


def _block_diag(t):
    l, g, a, b = t.shape
    eye = jnp.eye(g, dtype=bool)
    out = jnp.where(eye[None, :, None, :, None], t[:, :, :, None, :], jnp.zeros((), t.dtype))
    return out.reshape(l, g * a, g * b)


def _prep_weights(p):
    depth, d, _ = p["w_in"].shape
    w = BRANCH_WIDTH
    nh = N_HEADS
    w_in = p["w_in"]
    off_fb = 6 * w
    off_s5 = off_fb + nh
    off_beta = off_s5 + w + 4 * w
    off_gate = off_beta + 2 * nh
    wm = jnp.concatenate([w_in[:, :, 0:off_fb], w_in[:, :, off_s5:off_beta], w_in[:, :, off_gate:]], axis=-1)
    ws = jnp.concatenate([w_in[:, :, off_fb:off_s5], w_in[:, :, off_beta:off_gate],
                          jnp.zeros((depth, d, SMALL_COLS - 3 * nh), _F32)], axis=-1)
    tile_h = lambda g: jnp.tile(g, (1, nh))
    zeros_w = jnp.zeros((depth, w), _F32)
    gains = jnp.stack([tile_h(p["moba_qn_g"]), tile_h(p["moba_kn_g"]), tile_h(p["fox_qn_g"]), tile_h(p["fox_kn_g"]),
                       zeros_w, zeros_w, zeros_w, zeros_w], axis=1)
    z4 = jnp.zeros((depth, nh), _F32)
    pad = jnp.zeros((depth, SMALL_COLS - 3 * nh), _F32)
    sp_bias = jnp.concatenate([p["fox_f_bias"], z4, p["gdn_dt_bias"], pad], axis=-1)
    sp_alog = jnp.concatenate([z4, z4, p["gdn_a_log"], pad], axis=-1)
    sp = jnp.concatenate([sp_bias[:, None], sp_alog[:, None], jnp.zeros((depth, 6, SMALL_COLS), _F32)], axis=1)
    wkvt = jnp.concatenate([wm[:, :, w:3 * w], wm[:, :, 4 * w:6 * w]], axis=-1).transpose(0, 2, 1)
    wst = ws[:, :, 0:16].transpose(0, 2, 1)
    zeros_col = jnp.zeros((depth, w, 1), _F32)
    gains_t = jnp.concatenate([tile_h(p["moba_kn_g"])[:, :, None], tile_h(p["fox_kn_g"])[:, :, None]]
                              + [zeros_col] * 6, axis=-1)
    sp_t = jnp.concatenate([sp[:, 0:2, 0:16].transpose(0, 2, 1), jnp.zeros((depth, 16, 6), _F32)], axis=-1)

    g, s = p["s5_lambda_re"].shape[1:]
    flat = lambda a: a.reshape(depth * g, -1)
    ldt = jnp.broadcast_to(p["s5_log_dt"][:, :, None], (depth, g, s))
    lbre, lbim, fre, fim = _whole_call(_s5_disc_kernel, 4, (depth * g, s), "s5_disc",
                                       flat(p["s5_lambda_re"]), flat(p["s5_lambda_im"]), flat(ldt))
    rep = lambda a: jnp.repeat(a, S5_GROUP, axis=-1)
    bbre, bbim = _whole_call(_s5_bbar_kernel, 2, (depth * g, s * S5_GROUP), "s5_bbar",
                             rep(fre), rep(fim), flat(p["s5_b_re"]), flat(p["s5_b_im"]))
    to_hp = lambda a: a.reshape(depth, g, s, S5_GROUP).transpose(0, 1, 3, 2)
    bbd = jnp.concatenate([_block_diag(to_hp(bbre)), _block_diag(to_hp(bbim))], axis=-1).astype(_BF16)
    to_ph = lambda a: a.transpose(0, 1, 3, 2)
    cre = _block_diag(to_ph(p["s5_c_re"])).astype(_BF16)
    cim = _block_diag(to_ph(p["s5_c_im"])).astype(_BF16)

    conv_w = jnp.concatenate([p["gdn_conv_w"], jnp.zeros((depth, 8 - GDN_CONV, 3 * w), _F32)], axis=1)
    return dict(
        g1=p["norm1_g"][:, None, :], g2=p["norm2_g"][:, None, :],
        wm=wm.astype(_BF16), ws=ws.astype(_BF16), gains=gains, sp=sp,
        wkvt=wkvt.astype(_BF16), wst=wst.astype(_BF16), gains_t=gains_t, sp_t=sp_t,
        lbre=lbre.reshape(depth, 1, g * s), lbim=lbim.reshape(depth, 1, g * s), bbd=bbd, cre=cre, cim=cim,
        dskip=p["s5_d"][:, None, :], glu=p["s5_w_glu"].astype(_BF16),
        conv_w=conv_w, on_gain=p["gdn_out_g"][:, None, :],
        wb=p["w_branch"].astype(_BF16), wo=p["w_out"].astype(_BF16),
        wi=p["mlp_in"].astype(_BF16), wo2=p["mlp_out"].astype(_BF16),
    )


def _layer_prompt(x2d, mod, lw, cst, batch, seq):
    t = batch * seq
    w = BRANCH_WIDTH
    tm = 256
    r3 = lambda a: a.reshape(batch, seq, a.shape[-1])
    qa, kat, vat, qb, kbt, vbt, u, qkvd, zd, gates, sm, smt = _inproj_call(
        x2d, mod, lw, cst["seg"], batch=batch, seq=seq, tm=tm, transpose_kv=True)
    oa = _moba_prompt_call(r3(qa), kat, vat)
    cum = _cumsum_rows_call(smt, cst["tri_u"])
    ob = _fox_prompt_call(r3(qb), kbt, vbt, cum, tq=256)
    n = lw["lbre"].shape[-1]
    u_tm = r3(u).transpose(1, 0, 2).reshape(t, w)
    h0 = jnp.zeros((batch, n), _F32)
    oc_tm, sre, sim = _s5_call(u_tm, h0, h0, lw["lbre"], lw["lbim"], lw["bbd"], lw["cre"], lw["cim"], lw["dskip"],
                               lw["glu"], steps=64)
    oc = oc_tm.reshape(seq, batch, w).transpose(1, 0, 2).reshape(t, w)
    od, gstate = _gdn_prompt_call(r3(qkvd), r3(zd), r3(sm), lw["conv_w"], lw["on_gain"], cst["seg"], cst["tri_l"])
    x1 = _merge_call(oa.reshape(t, w), ob.reshape(t, w), oc, od.reshape(t, w), gates, x2d, mod, lw["wb"], lw["wo"],
                     seq=seq, tm=tm)
    x2 = _mlp_call(x1, mod, lw["g2"], lw["wi"], lw["wo2"], seq=seq, tm=tm)
    groups = n // S5_STATE
    new = dict(moba_k=kat, moba_v=vat, fox_k=kbt, fox_v=vbt, fox_logf=smt[:, 0:N_HEADS, :],
               s5_re=sre.reshape(batch, groups, S5_STATE), s5_im=sim.reshape(batch, groups, S5_STATE),
               gdn=gstate, gdn_conv=r3(qkvd)[:, seq - (GDN_CONV - 1):, :])
    return x2, new


def _layer_sample(x2d, mod, lw, cst, layer, caches, page_table, past):
    bs = x2d.shape[0]
    w = BRANCH_WIDTH
    nh = N_HEADS
    past_len = page_table.shape[1] * LANES
    qa, ka, va, qb, kb, vb, u, qkvd, zd, gates, sm = _inproj_call(
        x2d, mod, lw, cst["seg"], batch=bs, seq=1, tm=bs, transpose_kv=False)
    r3 = lambda a: a.reshape(bs, 1, a.shape[-1])
    r4 = lambda a: a.reshape(bs, nh, 1, HEAD_DIM)
    picked = _moba_select_call(layer, r3(qa), caches["moba_k"], page_table, cst["ones"])
    sel = picked[:, 0:nh, 0:MOBA_TOPK].reshape(bs, nh * MOBA_TOPK)
    oa = _moba_decode_call(layer, r4(qa), r4(ka), r4(va), sel, caches["moba_k"], caches["moba_v"], page_table,
                           cst["slopes"], past_len)
    ob = _fox_decode_call(layer, r3(qb), r3(kb), r3(vb), r3(sm), caches["fox_k"], caches["fox_v"], caches["fox_logf"],
                          page_table, cst["suf"])
    n = lw["lbre"].shape[-1]
    oc, sre, sim = _s5_call(u, past["s5_re"].reshape(bs, n), past["s5_im"].reshape(bs, n), lw["lbre"], lw["lbim"],
                            lw["bbd"], lw["cre"], lw["cim"], lw["dskip"], lw["glu"], steps=1)
    od, gstate = _gdn_decode_call(r3(qkvd), past["gdn_conv"], r3(zd), r3(sm), lw["conv_w"], lw["on_gain"], past["gdn"])
    x1 = _merge_call(oa.reshape(bs, w), ob.reshape(bs, w), oc, od.reshape(bs, w), gates, x2d, mod, lw["wb"], lw["wo"],
                     seq=1, tm=bs)
    x2 = _mlp_call(x1, mod, lw["g2"], lw["wi"], lw["wo2"], seq=1, tm=bs)
    heads = lambda a: a.reshape(bs, 1, nh, HEAD_DIM)
    new = dict(moba_k=heads(ka), moba_v=heads(va), fox_k=heads(kb), fox_v=heads(vb),
               fox_logf=sm[:, 0:nh].reshape(bs, 1, nh),
               s5_re=sre.reshape(past["s5_re"].shape), s5_im=sim.reshape(past["s5_im"].shape), gdn=gstate,
               gdn_conv=jnp.concatenate([past["gdn_conv"][:, 1:], r3(qkvd)], axis=1))
    return x2, new


_STATE_KEYS = ("moba_k", "moba_v", "fox_k", "fox_v", "fox_logf", "s5_re", "s5_im", "gdn", "gdn_conv")


def kernel(x_prompt, x_sample, cache_moba_k, cache_moba_v, cache_fox_k, cache_fox_v, cache_fox_logf, state_s5_re, state_s5_im, state_gdn, state_gdn_conv, page_table, c_prompt, c_sample, norm1_g, norm2_g, ada_w, ada_b, w_in, moba_qn_g, moba_kn_g, fox_qn_g, fox_kn_g, fox_f_bias, s5_lambda_re, s5_lambda_im, s5_b_re, s5_b_im, s5_c_re, s5_c_im, s5_d, s5_log_dt, s5_w_glu, gdn_conv_w, gdn_a_log, gdn_dt_bias, gdn_out_g, w_branch, w_out, mlp_in, mlp_out):
    batch, seq, d = x_prompt.shape
    bs, dec_seq, _ = x_sample.shape
    depth = w_in.shape[0]
    assert dec_seq == 1 and page_table.shape == (bs, page_table.shape[1]) and cache_moba_k.shape[2] == LANES
    assert (batch + bs) % 8 == 0 and batch % 8 == 0 and bs % 8 == 0
    weights = dict(norm1_g=norm1_g, norm2_g=norm2_g, w_in=w_in, moba_qn_g=moba_qn_g, moba_kn_g=moba_kn_g,
                   fox_qn_g=fox_qn_g, fox_kn_g=fox_kn_g, fox_f_bias=fox_f_bias, s5_lambda_re=s5_lambda_re,
                   s5_lambda_im=s5_lambda_im, s5_b_re=s5_b_re, s5_b_im=s5_b_im, s5_c_re=s5_c_re, s5_c_im=s5_c_im,
                   s5_d=s5_d, s5_log_dt=s5_log_dt, s5_w_glu=s5_w_glu, gdn_conv_w=gdn_conv_w, gdn_a_log=gdn_a_log,
                   gdn_dt_bias=gdn_dt_bias, gdn_out_g=gdn_out_g, w_branch=w_branch, w_out=w_out, mlp_in=mlp_in,
                   mlp_out=mlp_out)
    lw_all = _prep_weights(weights)
    cst = _constants()
    mod_all = _ada_call(jnp.concatenate([c_prompt, c_sample], axis=0), ada_w, ada_b)

    pages_t = lambda c: c.transpose(0, 1, 3, 4, 2).reshape(c.shape[0], c.shape[1], BRANCH_WIDTH, LANES)
    logf_t = jnp.pad(cache_fox_logf.transpose(0, 1, 3, 2), ((0, 0), (0, 0), (0, 8 - N_HEADS), (0, 0)))
    caches = dict(moba_k=pages_t(cache_moba_k), moba_v=pages_t(cache_moba_v), fox_k=pages_t(cache_fox_k),
                  fox_v=pages_t(cache_fox_v), fox_logf=logf_t)

    xp = x_prompt.reshape(batch * seq, d)
    xs = x_sample.reshape(bs, d)
    new_p = {k: [] for k in _STATE_KEYS}
    new_s = {k: [] for k in _STATE_KEYS}
    for layer in range(depth):
        lw = {k: v[layer] for k, v in lw_all.items()}
        xp, st_p = _layer_prompt(xp, mod_all[layer, 0:batch][:, None, :], lw, cst, batch, seq)
        past = dict(s5_re=state_s5_re[layer], s5_im=state_s5_im[layer], gdn=state_gdn[layer],
                    gdn_conv=state_gdn_conv[layer])
        xs, st_s = _layer_sample(xs, mod_all[layer, batch:], lw, cst, layer, caches, page_table, past)
        for k in _STATE_KEYS:
            new_p[k].append(st_p[k])
            new_s[k].append(st_s[k])
    pn = {k: jnp.stack(v) for k, v in new_p.items()}
    sn = {k: jnp.stack(v) for k, v in new_s.items()}
    heads_last = lambda a: a.reshape(depth, batch, N_HEADS, HEAD_DIM, seq).transpose(0, 1, 4, 2, 3)
    return (xp.reshape(batch, seq, d), xs.reshape(bs, 1, d),
            heads_last(pn["moba_k"]), heads_last(pn["moba_v"]), heads_last(pn["fox_k"]), heads_last(pn["fox_v"]),
            pn["fox_logf"].transpose(0, 1, 3, 2), pn["s5_re"], pn["s5_im"], pn["gdn"], pn["gdn_conv"],
            sn["moba_k"], sn["moba_v"], sn["fox_k"], sn["fox_v"], sn["fox_logf"],
            sn["s5_re"], sn["s5_im"], sn["gdn"], sn["gdn_conv"])
```

```python
import functools

import jax
import jax.numpy as jnp
from jax import lax
from jax.experimental import pallas as pl
from jax.experimental.pallas import tpu as pltpu

HEAD_DIM = 64
N_BRANCH = 4
N_HEADS = 4
BRANCH_WIDTH = N_HEADS * HEAD_DIM
MOBA_BLOCK = 256
MOBA_TOPK = 3
S5_GROUP = 16
S5_STATE = 64
GDN_CONV = 4
GDN_CHUNK = 64
N_ADA = 6
EPS = 1e-6
NEG_INF = -1e30
LANES = 128
SMALL_COLS = 128
VMEM_LIMIT_BYTES = 56 * 1024 * 1024

_F32 = jnp.float32
_BF16 = jnp.bfloat16
_HIGHEST = lax.Precision.HIGHEST
_NT = (((1,), (1,)), ((), ()))


def _params(*semantics):
    return pltpu.CompilerParams(dimension_semantics=semantics, vmem_limit_bytes=VMEM_LIMIT_BYTES)


def _dot(a, b):
    return jnp.dot(a.astype(_BF16), b.astype(_BF16), preferred_element_type=_F32)


def _dot_nt(a, b):
    return lax.dot_general(a.astype(_BF16), b.astype(_BF16), _NT, preferred_element_type=_F32)


def _split3(x):
    hi = x.astype(_BF16)
    r1 = x - hi.astype(_F32)
    mid = r1.astype(_BF16)
    lo = (r1 - mid.astype(_F32)).astype(_BF16)
    return hi, mid, lo


def _dot_x_exact(x, m):
    hi, mid, lo = _split3(x)
    d = lambda a: jnp.dot(a, m, preferred_element_type=_F32)
    return d(hi) + d(mid) + d(lo)


def _exact_dot_x(m, x):
    hi, mid, lo = _split3(x)
    d = lambda a: jnp.dot(m, a, preferred_element_type=_F32)
    return d(hi) + d(mid) + d(lo)


def _const_spec(shape):
    nd = len(shape)
    return pl.BlockSpec(shape, lambda *_: (0,) * nd)


def _ada_kernel(c_ref, w_ref, b_ref, o_ref):
    c = c_ref[...]
    o_ref[...] = _dot_3pass(c * jax.nn.sigmoid(c), w_ref[...]) + b_ref[...]


def _ada_call(c_all, ada_w, ada_b):
    depth, d, n = ada_w.shape
    rows = c_all.shape[0]
    tn = n // 4
    return pl.pallas_call(
        _ada_kernel,
        grid=(depth, n // tn),
        in_specs=[pl.BlockSpec((rows, d), lambda l, j: (0, 0)),
                  pl.BlockSpec((None, d, tn), lambda l, j: (l, 0, j)),
                  pl.BlockSpec((None, 1, tn), lambda l, j: (l, 0, j))],
        out_specs=pl.BlockSpec((None, rows, tn), lambda l, j: (l, 0, j)),
        out_shape=jax.ShapeDtypeStruct((depth, rows, n), _F32),
        compiler_params=_params("parallel", "parallel"),
        name="ada_mod",
    )(c_all, ada_w, ada_b.reshape(depth, 1, n))


_N_SLABS = 11 + 4 * N_BRANCH


def _scalar_gates(v_raw, bias, alog, index):
    v = v_raw + bias
    logf = jax.nn.log_sigmoid(v)
    beta = jax.nn.sigmoid(v_raw)
    g = -jnp.exp(alog) * jax.nn.softplus(v)
    nh = N_HEADS
    return jnp.where(index < nh, logf, jnp.where(index < 2 * nh, beta, jnp.where(index < 3 * nh, g, 0.0)))


def _mod_spec(mod, seq, tm):
    if seq > 1:
        tiles_per_batch = seq // tm
        return pl.BlockSpec((None, 1, mod.shape[-1]), lambda i: (i // tiles_per_batch, 0, 0))
    return pl.BlockSpec((tm, mod.shape[-1]), lambda i: (i, 0))


def _modulated_norm(x_ref, mod_ref, g_ref, shift_chunk, scale_chunk):
    d = x_ref.shape[1]
    x = x_ref[...]
    y = x * lax.rsqrt(jnp.mean(x * x, axis=-1, keepdims=True) + EPS) * g_ref[...]
    return y * (1.0 + mod_ref[:, scale_chunk * d:(scale_chunk + 1) * d]) + mod_ref[:, shift_chunk * d:(shift_chunk + 1) * d]


def _head_norm_rows(p, seg_ref, gain_row):
    ss = _dot_x_exact(p * p, seg_ref[...])
    return p * lax.rsqrt(ss * (1.0 / HEAD_DIM) + EPS) * gain_row


def _inproj_common(slab, small, sp_ref, u_ref, qkvd_ref, zd_ref, gates_ref, sm_ref):
    w = BRANCH_WIDTH
    u_ref[...] = slab(6)
    for i in range(3):
        qkvd_ref[:, i * w:(i + 1) * w] = slab(7 + i)
    zd_ref[...] = slab(10)
    for i in range(4 * N_BRANCH):
        gates_ref[:, i * w:(i + 1) * w] = slab(11 + i)
    out = _scalar_gates(small, sp_ref[0:1, :], sp_ref[1:2, :], lax.broadcasted_iota(jnp.int32, small.shape, 1))
    sm_ref[...] = out
    return out


def _inproj_sample_kernel(x_ref, mod_ref, g1_ref, wm_ref, wml_ref, ws_ref, wsl_ref, gains_ref, sp_ref, seg_ref,
                          qa_ref, ka_ref, va_ref, qb_ref, kb_ref, vb_ref, u_ref, qkvd_ref, zd_ref, gates_ref, sm_ref):
    w = BRANCH_WIDTH
    h = _modulated_norm(x_ref, mod_ref, g1_ref, 0, 1)
    slab = lambda s: _dot_hi_lo(h, wm_ref[:, s * w:(s + 1) * w], wml_ref[:, s * w:(s + 1) * w])
    _inproj_common(slab, _dot_hi_lo(h, ws_ref[...], wsl_ref[...]), sp_ref, u_ref, qkvd_ref, zd_ref, gates_ref, sm_ref)
    qa_ref[...] = _head_norm_rows(slab(0), seg_ref, gains_ref[0:1, :])
    ka_ref[...] = _head_norm_rows(slab(1), seg_ref, gains_ref[1:2, :])
    va_ref[...] = slab(2)
    qb_ref[...] = _head_norm_rows(slab(3), seg_ref, gains_ref[2:3, :])
    kb_ref[...] = _head_norm_rows(slab(4), seg_ref, gains_ref[3:4, :])
    vb_ref[...] = slab(5)


def _inproj_sample_call(x2d, mod, lw, seg):
    t, d = x2d.shape
    w = BRANCH_WIDTH
    row = lambda n: pl.BlockSpec((t, n), lambda i: (i, 0))
    widths = [w] * 7 + [3 * w, w, 4 * N_BRANCH * w, SMALL_COLS]
    args = [x2d, mod, lw["g1"], lw["wm"], lw["wm_lo"], lw["ws"], lw["ws_lo"], lw["gains"], lw["sp"], seg]
    return pl.pallas_call(
        _inproj_sample_kernel,
        grid=(1,),
        in_specs=[row(d), _mod_spec(mod, 1, t)] + [_const_spec(a.shape) for a in args[2:]],
        out_specs=[row(n) for n in widths],
        out_shape=[jax.ShapeDtypeStruct((t, n), _F32) for n in widths],
        compiler_params=_params("arbitrary"),
        name="inproj_sample",
    )(*args)


FOX_BIAS_TERMS = 3


def _inproj_prompt_kernel(x_ref, mod_ref, g1_ref, wm_ref, ws_ref, gains_ref, sp_ref, seg_ref, wt_ref, wst_ref, gt_ref,
                          spt_ref, tri_ref,
                          qat_ref, kat_ref, vat_ref, qbt_ref, kbt_ref, vbt_ref, kar_ref, kmean_ref, kbaug_ref,
                          u_ref, qkvd_ref, zd_ref, gates_ref, sm_ref, smt_ref, cum_ref):
    w = BRANCH_WIDTH
    hb = _modulated_norm(x_ref, mod_ref, g1_ref, 0, 1).astype(_BF16)
    slab = lambda s: jnp.dot(hb, wm_ref[:, s * w:(s + 1) * w], preferred_element_type=_F32)
    sm = _inproj_common(slab, jnp.dot(hb, ws_ref[...], preferred_element_type=_F32), sp_ref,
                        u_ref, qkvd_ref, zd_ref, gates_ref, sm_ref)

    def slab_t(i):
        return lax.dot_general(wt_ref[i * w:(i + 1) * w, :], hb, _NT, preferred_element_type=_F32)

    def put_head_norm_t(ref, p, col):
        for hd in range(N_HEADS):
            hs = slice(hd * HEAD_DIM, (hd + 1) * HEAD_DIM)
            ph = p[hs, :]
            ms = jnp.mean(ph * ph, axis=0, keepdims=True)
            ref[hs, :] = ph * lax.rsqrt(ms + EPS) * gt_ref[hs, col:col + 1]

    put_head_norm_t(qat_ref, slab_t(0), 0)
    put_head_norm_t(kat_ref, slab_t(1), 1)
    vat_ref[...] = slab_t(2)
    put_head_norm_t(qbt_ref, slab_t(3), 2)
    put_head_norm_t(kbt_ref, slab_t(4), 3)
    vbt_ref[...] = slab_t(5)
    smt = lax.dot_general(wst_ref[...], hb, _NT, preferred_element_type=_F32)
    smt_ref[...] = _scalar_gates(smt, spt_ref[:, 0:1], spt_ref[:, 1:2], lax.broadcasted_iota(jnp.int32, smt.shape, 0))

    ka = _head_norm_rows(slab(1), seg_ref, gains_ref[1:2, :])
    kmean_ref[...] = jnp.mean(ka, axis=0, keepdims=True)
    for hd in range(N_HEADS):
        kar_ref[hd] = ka[:, hd * HEAD_DIM:(hd + 1) * HEAD_DIM].astype(_BF16)

    @pl.when(pl.program_id(1) == 0)
    def _():
        cum_ref[...] = jnp.zeros(cum_ref.shape, _F32)

    cum = _exact_dot_x(tri_ref[...], sm) + cum_ref[...]
    cum_ref[...] = cum[cum.shape[0] - 1:, :]
    kb = _head_norm_rows(slab(4), seg_ref, gains_ref[3:4, :])
    lane = lax.broadcasted_iota(jnp.int32, (kb.shape[0], LANES), 1)
    for hd in range(N_HEADS):
        pair = kb[:, (hd // 2) * LANES:(hd // 2 + 1) * LANES]
        if hd % 2:
            pair = pltpu.roll(pair, HEAD_DIM, axis=1)
        aug = jnp.where(lane < HEAD_DIM, pair, 0.0)
        for i, term in enumerate(_split3(-cum[:, hd:hd + 1])):
            aug = jnp.where(lane == HEAD_DIM + i, term.astype(_F32), aug)
        kbaug_ref[hd] = aug.astype(_BF16)


def _inproj_prompt_call(x2d, mod, lw, seg, tri, *, batch, seq):
    t, d = x2d.shape
    w = BRANCH_WIDTH
    tm = MOBA_BLOCK
    assert seq % tm == 0
    nb = seq // tm
    row = lambda n: pl.BlockSpec((tm, n), lambda b, j: (b * nb + j, 0))
    rs = lambda n: jax.ShapeDtypeStruct((t, n), _F32)
    t_spec = lambda r: pl.BlockSpec((None, r, tm), lambda b, j: (b, 0, j))
    t_shape = lambda r: jax.ShapeDtypeStruct((batch, r, seq), _F32)
    head_rows = lambda n: pl.BlockSpec((None, N_HEADS, tm, n), lambda b, j: (b, 0, j, 0))
    args = [x2d, mod, lw["g1"], lw["wm"], lw["ws"], lw["gains"], lw["sp"], seg, lw["wt"], lw["wst"], lw["gains_t"],
            lw["sp_t"], tri]
    out_specs = [t_spec(w)] * 6 + [head_rows(HEAD_DIM), pl.BlockSpec((None, None, 1, w), lambda b, j: (b, j, 0, 0)),
                                   head_rows(LANES), row(w), row(3 * w), row(w), row(4 * N_BRANCH * w), row(SMALL_COLS),
                                   t_spec(16)]
    out_shape = [t_shape(w)] * 6 + [jax.ShapeDtypeStruct((batch, N_HEADS, seq, HEAD_DIM), _BF16),
                                    jax.ShapeDtypeStruct((batch, nb, 1, w), _F32),
                                    jax.ShapeDtypeStruct((batch, N_HEADS, seq, LANES), _BF16),
                                    rs(w), rs(3 * w), rs(w), rs(4 * N_BRANCH * w), rs(SMALL_COLS), t_shape(16)]
    return pl.pallas_call(
        _inproj_prompt_kernel,
        grid=(batch, nb),
        in_specs=[row(d), pl.BlockSpec((None, 1, mod.shape[-1]), lambda b, j: (b, 0, 0))]
        + [_const_spec(a.shape) for a in args[2:]],
        out_specs=out_specs,
        out_shape=out_shape,
        scratch_shapes=[pltpu.VMEM((1, SMALL_COLS), _F32)],
        compiler_params=_params("parallel", "arbitrary"),
        name="inproj_prompt",
    )(*args)


def _alibi_slope(h):
    return float(2.0 ** (-8.0 * (h + 1) / N_HEADS))


def _init_softmax_state(m_ref, l_ref, acc_ref):
    m_ref[...] = jnp.full(m_ref.shape, NEG_INF, _F32)
    l_ref[...] = jnp.zeros(l_ref.shape, _F32)
    acc_ref[...] = jnp.zeros(acc_ref.shape, _F32)


def _softmax_step_t(h, s, vt, m_ref, l_ref, acc_ref):
    m_old = m_ref[h]
    m_new = jnp.maximum(m_old, jnp.max(s, axis=0, keepdims=True))
    alpha = jnp.exp(m_old - m_new)
    p = jnp.exp(s - m_new)
    l_ref[h] = alpha * l_ref[h] + jnp.sum(p, axis=0, keepdims=True)
    acc_ref[h] = alpha * acc_ref[h] + _dot(vt, p)
    m_ref[h] = m_new


def _softmax_scratch(tq):
    return [pltpu.VMEM((N_HEADS, 1, tq), _F32), pltpu.VMEM((N_HEADS, 1, tq), _F32),
            pltpu.VMEM((N_HEADS, HEAD_DIM, tq), _F32)]


def _write_attention_out(o_ref, l_ref, acc_ref):
    out_t = jnp.concatenate([acc_ref[h] / l_ref[h] for h in range(N_HEADS)], axis=0)
    o_ref[...] = out_t.T


def _moba_prompt_kernel(qt_ref, kr_ref, vt_ref, km_ref, o_ref, sel_ref, m_ref, l_ref, acc_ref, *, nb):
    qi = pl.program_id(1)
    tq = qt_ref.shape[1]
    blk = MOBA_BLOCK
    nbp = km_ref.shape[0]
    blk_row = lax.broadcasted_iota(jnp.int32, (nbp, tq), 0)
    rc = (lax.broadcasted_iota(jnp.int32, (blk, tq), 1) - lax.broadcasted_iota(jnp.int32, (blk, tq), 0)).astype(_F32)
    qs = []
    for h in range(N_HEADS):
        hs = slice(h * HEAD_DIM, (h + 1) * HEAD_DIM)
        qt = qt_ref[hs, :]
        gate = jnp.dot(km_ref[:, hs], qt, precision=_HIGHEST, preferred_element_type=_F32)
        valid = blk_row < qi
        gv = jnp.where(valid, gate, NEG_INF)
        cnt = jnp.zeros((nbp, tq), _F32)
        for jp in range(nb):
            other = gv[jp:jp + 1, :]
            cnt = cnt + ((other > gv) | ((other == gv) & (jp < blk_row))).astype(_F32)
        sel_ref[h] = (valid & (cnt < MOBA_TOPK)).astype(_F32)
        qs.append((qt * (HEAD_DIM ** -0.5)).astype(_BF16))
    _init_softmax_state(m_ref, l_ref, acc_ref)

    def block(j, own):
        start = pl.multiple_of(j * blk, blk)
        dist = rc + ((qi - j) * blk).astype(_F32)
        for h in range(N_HEADS):
            hs = slice(h * HEAD_DIM, (h + 1) * HEAD_DIM)
            s = jnp.dot(kr_ref[h, pl.ds(start, blk), :], qs[h], preferred_element_type=_F32)
            s = s - _alibi_slope(h) * dist
            keep = (rc >= 0) if own else (sel_ref[h, pl.ds(j, 1), :] > 0.5)
            _softmax_step_t(h, jnp.where(keep, s, NEG_INF), vt_ref[hs, pl.ds(start, blk)], m_ref, l_ref, acc_ref)

    block(qi, True)
    lax.fori_loop(0, qi, lambda j, c: (block(j, False), c)[1], 0)
    _write_attention_out(o_ref, l_ref, acc_ref)


def _moba_prompt_call(qt, kr, vt, kmean):
    b, w, s = qt.shape
    assert s % MOBA_BLOCK == 0
    nb = s // MOBA_BLOCK
    nbp = kmean.shape[1]
    return pl.pallas_call(
        functools.partial(_moba_prompt_kernel, nb=nb),
        grid=(b, nb),
        in_specs=[pl.BlockSpec((None, w, MOBA_BLOCK), lambda i, j: (i, 0, j)),
                  pl.BlockSpec((None, N_HEADS, s, HEAD_DIM), lambda i, j: (i, 0, 0, 0)),
                  pl.BlockSpec((None, w, s), lambda i, j: (i, 0, 0)),
                  pl.BlockSpec((None, nbp, w), lambda i, j: (i, 0, 0))],
        out_specs=pl.BlockSpec((None, MOBA_BLOCK, w), lambda i, j: (i, j, 0)),
        out_shape=jax.ShapeDtypeStruct((b, s, w), _F32),
        scratch_shapes=[pltpu.VMEM((N_HEADS, nbp, MOBA_BLOCK), _F32)] + _softmax_scratch(MOBA_BLOCK),
        compiler_params=_params("parallel", "parallel"),
        name="moba_prompt",
    )(qt, kr, vt, kmean)


def _fox_prompt_kernel(qt_ref, kaug_ref, vt_ref, o_ref, m_ref, l_ref, acc_ref):
    qi = pl.program_id(1)
    tq = qt_ref.shape[1]
    blk = tq
    causal = lax.broadcasted_iota(jnp.int32, (blk, tq), 1) >= lax.broadcasted_iota(jnp.int32, (blk, tq), 0)
    ones_rows = (lax.broadcasted_iota(jnp.int32, (LANES - HEAD_DIM, tq), 0) < FOX_BIAS_TERMS).astype(_BF16)
    qs = []
    for h in range(N_HEADS):
        qt = (qt_ref[h * HEAD_DIM:(h + 1) * HEAD_DIM, :] * (HEAD_DIM ** -0.5)).astype(_BF16)
        qs.append(jnp.concatenate([qt, ones_rows], axis=0))
    _init_softmax_state(m_ref, l_ref, acc_ref)

    def block(j, own):
        start = pl.multiple_of(j * blk, blk)
        for h in range(N_HEADS):
            hs = slice(h * HEAD_DIM, (h + 1) * HEAD_DIM)
            s = jnp.dot(kaug_ref[h, pl.ds(start, blk), :], qs[h], preferred_element_type=_F32)
            if own:
                s = jnp.where(causal, s, NEG_INF)
            _softmax_step_t(h, s, vt_ref[hs, pl.ds(start, blk)], m_ref, l_ref, acc_ref)

    block(qi, True)
    lax.fori_loop(0, qi, lambda j, c: (block(j, False), c)[1], 0)
    _write_attention_out(o_ref, l_ref, acc_ref)


def _fox_prompt_call(qt, kaug, vt):
    b, w, s = qt.shape
    tq = MOBA_BLOCK
    assert s % tq == 0
    return pl.pallas_call(
        _fox_prompt_kernel,
        grid=(b, s // tq),
        in_specs=[pl.BlockSpec((None, w, tq), lambda i, j: (i, 0, j)),
                  pl.BlockSpec((None, N_HEADS, s, LANES), lambda i, j: (i, 0, 0, 0)),
                  pl.BlockSpec((None, w, s), lambda i, j: (i, 0, 0))],
        out_specs=pl.BlockSpec((None, tq, w), lambda i, j: (i, j, 0)),
        out_shape=jax.ShapeDtypeStruct((b, s, w), _F32),
        scratch_shapes=_softmax_scratch(tq),
        compiler_params=_params("parallel", "parallel"),
        name="fox_prompt",
    )(qt, kaug, vt)


def _s5_disc_kernel(lre_ref, lim_ref, ldt_ref, lbre_ref, lbim_ref, fre_ref, fim_ref):
    lre = lre_ref[...]
    lim = lim_ref[...]
    dt = jnp.exp(ldt_ref[...])
    mag = jnp.exp(lre * dt)
    lb_re = mag * jnp.cos(lim * dt)
    lb_im = mag * jnp.sin(lim * dt)
    den = lre * lre + lim * lim
    nr = lb_re - 1.0
    lbre_ref[...] = lb_re
    lbim_ref[...] = lb_im
    fre_ref[...] = (nr * lre + lb_im * lim) / den
    fim_ref[...] = (lb_im * lre - nr * lim) / den


def _s5_bbar_kernel(fre_ref, fim_ref, bre_ref, bim_ref, ore_ref, oim_ref):
    fre, fim, bre, bim = fre_ref[...], fim_ref[...], bre_ref[...], bim_ref[...]
    ore_ref[...] = fre * bre - fim * bim
    oim_ref[...] = fre * bim + fim * bre


def _whole_call(kernel, n_out, shape, name, *args):
    return pl.pallas_call(
        kernel,
        out_shape=[jax.ShapeDtypeStruct(shape, _F32)] * n_out,
        name=name,
    )(*args)


def _s5_kernel(u_ref, h0re_ref, h0im_ref, lbre_ref, lbim_ref, d_ref, *refs, steps, bp):
    n_w = (len(refs) - 5) // 4
    bbd_refs, cre_refs, cim_refs, glu_refs = (refs[i * n_w:(i + 1) * n_w] for i in range(4))
    oc_ref, sre_ref, sim_ref, bu_ref, st_ref = refs[4 * n_w:]
    n = lbre_ref.shape[1]

    @pl.when(pl.program_id(0) == 0)
    def _():
        st_ref[0] = h0re_ref[...]
        st_ref[1] = h0im_ref[...]

    u = u_ref[...]
    bu_ref[...] = _matmul_w(u, bbd_refs)
    a_re = jnp.broadcast_to(lbre_ref[...], (bp, n))
    a_im = jnp.broadcast_to(lbim_ref[...], (bp, n))

    def step(t, carry):
        xr, xi = carry
        rows = pl.ds(pl.multiple_of(t * bp, bp), bp)
        nr = a_re * xr - a_im * xi + bu_ref[rows, 0:n]
        ni = a_re * xi + a_im * xr + bu_ref[rows, n:2 * n]
        bu_ref[rows, 0:n] = nr
        bu_ref[rows, n:2 * n] = ni
        return nr, ni

    xr, xi = lax.fori_loop(0, steps, step, (st_ref[0], st_ref[1]))
    st_ref[0] = xr
    st_ref[1] = xi
    sre_ref[...] = xr
    sim_ref[...] = xi
    y = _matmul_w(bu_ref[:, 0:n], cre_refs) - _matmul_w(bu_ref[:, n:2 * n], cim_refs) + d_ref[...] * u
    yc = jax.nn.gelu(y)
    oc_ref[...] = yc * jax.nn.sigmoid(_matmul_w(yc, glu_refs))


def _s5_call(u_tm, h0re, h0im, lbre, lbim, dskip, bbd, cre, cim, glu, *, steps):
    rows, w = u_tm.shape
    bp, n = h0re.shape
    total_steps = rows // bp
    assert total_steps % steps == 0 and bp % 8 == 0
    r = steps * bp
    st = jax.ShapeDtypeStruct((bp, n), _F32)
    weights = list(bbd) + list(cre) + list(cim) + list(glu)
    return pl.pallas_call(
        functools.partial(_s5_kernel, steps=steps, bp=bp),
        grid=(total_steps // steps,),
        in_specs=[pl.BlockSpec((r, w), lambda c: (c, 0)), _const_spec((bp, n)), _const_spec((bp, n)),
                  _const_spec(lbre.shape), _const_spec(lbim.shape), _const_spec(dskip.shape)]
        + [_const_spec(a.shape) for a in weights],
        out_specs=[pl.BlockSpec((r, w), lambda c: (c, 0)), _const_spec((bp, n)), _const_spec((bp, n))],
        out_shape=[jax.ShapeDtypeStruct((rows, w), _F32), st, st],
        scratch_shapes=[pltpu.VMEM((r, 2 * n), _F32), pltpu.VMEM((2, bp, n), _F32)],
        compiler_params=_params("arbitrary"),
        name="s5_scan",
    )(u_tm, h0re, h0im, lbre, lbim, dskip, *weights)


def _split2(x):
    hi = x.astype(_BF16)
    return hi, (x - hi.astype(_F32)).astype(_BF16)


def _dot_3pass(a, b, dims=None):
    ah, al = _split2(a)
    bh, bl = _split2(b)
    if dims is None:
        d = lambda x, y: jnp.dot(x, y, preferred_element_type=_F32)
    else:
        d = lambda x, y: lax.dot_general(x, y, dims, preferred_element_type=_F32)
    return d(ah, bh) + d(ah, bl) + d(al, bh)


def _dot_hi_lo(x, w_hi, w_lo):
    xh, xl = _split2(x)
    d = lambda a, b: jnp.dot(a, b, preferred_element_type=_F32)
    return d(xh, w_hi) + d(xh, w_lo) + d(xl, w_hi)


def _unit_lower_inverse(nmat, eye):
    size = nmat.shape[0]
    t = eye - nmat
    power = nmat
    span = 2
    while span < size:
        power = _dot(power, power)
        t = t + _dot(t, power)
        span *= 2
    resid = eye - _dot_3pass(eye + nmat, t)
    return t + _dot(t, resid)


GDN_CHUNKS_PER_STEP = 4


def _gdn_prompt_kernel(x_ref, z_ref, sm_ref, cw_ref, on_ref, seg_ref, tri_ref,
                       o_ref, s_out_ref, ext_ref, state_ref):
    c = GDN_CHUNK
    w = BRANCH_WIDTH
    rows = x_ref.shape[0]
    n = pl.program_id(1)

    @pl.when(n == 0)
    def _():
        ext_ref[0:8, :] = jnp.zeros((8, 3 * w), _F32)
        state_ref[...] = jnp.zeros(state_ref.shape, _F32)

    ext_ref[8:8 + rows, :] = x_ref[...]
    conv = cw_ref[0:1, :] * ext_ref[5:5 + rows, :]
    for j in range(1, GDN_CONV):
        conv = conv + cw_ref[j:j + 1, :] * ext_ref[5 + j:5 + j + rows, :]
    ext_ref[0:8, :] = ext_ref[rows:rows + 8, :]
    act = conv * jax.nn.sigmoid(conv)
    l2 = lambda a: a * lax.rsqrt(_dot_x_exact(a * a, seg_ref[...]) + EPS)
    q_all = l2(act[:, 0:w]) * (HEAD_DIM ** -0.5)
    k_all = l2(act[:, w:2 * w])
    v_all = act[:, 2 * w:3 * w]
    sm_all = sm_ref[...]
    z_all = z_ref[...]
    st = w
    row = lax.broadcasted_iota(jnp.int32, (st, st), 0)
    col = lax.broadcasted_iota(jnp.int32, (st, st), 1)
    same_head = lax.shift_right_logical(row, 6) == lax.shift_right_logical(col, 6)
    eye = (row == col).astype(_F32)
    lower = same_head & (row >= col)
    strict = same_head & (row > col)
    stack = lambda a, rs: jnp.concatenate([a[rs, h * HEAD_DIM:(h + 1) * HEAD_DIM] for h in range(N_HEADS)], axis=0)
    stack_col = lambda a, first: jnp.concatenate([a[:, first + h:first + h + 1] for h in range(N_HEADS)], axis=0)
    block_diag = lambda a: jnp.where(same_head, jnp.concatenate([a] * N_HEADS, axis=1), 0.0)
    state = state_ref[...]
    for cc in range(rows // c):
        rs = slice(cc * c, (cc + 1) * c)
        sm = sm_all[rs, :]
        gc_all = _exact_dot_x(tri_ref[...], sm)
        q, k, v = stack(q_all, rs), stack(k_all, rs), stack(v_all, rs)
        beta = stack_col(sm, N_HEADS)
        gcol = stack_col(gc_all, 2 * N_HEADS)
        g_last = jnp.concatenate([jnp.broadcast_to(gc_all[c - 1:c, 2 * N_HEADS + h:2 * N_HEADS + h + 1], (c, 1))
                                  for h in range(N_HEADS)], axis=0)
        grow = jnp.sum(eye * gcol, axis=0, keepdims=True)
        decay = jnp.where(lower, jnp.exp(jnp.where(lower, gcol - grow, 0.0)), 0.0)
        kb = k * beta
        nmat = jnp.where(strict, _dot_nt(kb, k) * decay, 0.0)
        t = _unit_lower_inverse(nmat, eye)
        u = _dot(t, v * beta)
        wmat = _dot(t, kb * jnp.exp(gcol))
        attn = _dot_nt(q, k) * decay
        kdec_t = _dot_nt(eye, block_diag(k * jnp.exp(g_last - gcol)))
        v_new = u - _dot(block_diag(wmat), state)
        o = _dot(block_diag(q * jnp.exp(gcol)), state) + _dot(attn, v_new)
        state = state * jnp.exp(g_last) + _dot(kdec_t, v_new)
        on = o * lax.rsqrt(jnp.mean(o * o, axis=-1, keepdims=True) + EPS) * on_ref[...]
        for h in range(N_HEADS):
            hs = slice(h * HEAD_DIM, (h + 1) * HEAD_DIM)
            zh = z_all[rs, hs]
            o_ref[rs, hs] = on[h * c:(h + 1) * c, :] * (zh * jax.nn.sigmoid(zh))
    state_ref[...] = state
    s_out_ref[...] = state


def _gdn_prompt_call(qkvd, zd, sm_rows, conv_w, on_gain, seg, tri):
    b, s, w3 = qkvd.shape
    w = w3 // 3
    rows = GDN_CHUNK * GDN_CHUNKS_PER_STEP
    assert s % rows == 0 and HEAD_DIM == GDN_CHUNK and w == N_HEADS * HEAD_DIM
    blk = lambda n: pl.BlockSpec((None, rows, n), lambda i, j: (i, j, 0))
    od, state = pl.pallas_call(
        _gdn_prompt_kernel,
        grid=(b, s // rows),
        in_specs=[blk(w3), blk(w), blk(SMALL_COLS), _const_spec(conv_w.shape), _const_spec(on_gain.shape),
                  _const_spec(seg.shape), _const_spec(tri.shape)],
        out_specs=[blk(w), pl.BlockSpec((None, w, HEAD_DIM), lambda i, j: (i, 0, 0))],
        out_shape=[jax.ShapeDtypeStruct((b, s, w), _F32), jax.ShapeDtypeStruct((b, w, HEAD_DIM), _F32)],
        scratch_shapes=[pltpu.VMEM((rows + 8, w3), _F32), pltpu.VMEM((w, HEAD_DIM), _F32)],
        compiler_params=_params("parallel", "arbitrary"),
        name="gdn_prompt",
    )(qkvd, zd, sm_rows, conv_w, on_gain, seg, tri)
    return od, state.reshape(b, N_HEADS, HEAD_DIM, HEAD_DIM)


def _matmul_w(x, w_refs, index=None):
    pick = (lambda r: r[...]) if index is None else (lambda r: r[index])
    if len(w_refs) == 1:
        return _dot(x, pick(w_refs[0]))
    return _dot_hi_lo(x, pick(w_refs[0]), pick(w_refs[1]))


def _merge_kernel(oa_ref, ob_ref, oc_ref, od_ref, gates_ref, x_ref, mod_ref, *refs):
    n_w = (len(refs) - 1) // 2
    wb_refs, wo_refs, y_ref = refs[:n_w], refs[n_w:2 * n_w], refs[2 * n_w]
    d = x_ref.shape[1]
    mixed = None
    for i, ref in enumerate((oa_ref, ob_ref, oc_ref, od_ref)):
        term = jax.nn.sigmoid(gates_ref[:, i * d:(i + 1) * d]) * _matmul_w(ref[...], wb_refs, i)
        mixed = term if mixed is None else mixed + term
    y_ref[...] = x_ref[...] + mod_ref[:, 2 * d:3 * d] * _matmul_w(mixed, wo_refs)


def _mlp_kernel(x_ref, mod_ref, g2_ref, *refs):
    n_w = (len(refs) - 1) // 2
    wi_refs, wo_refs, y_ref = refs[:n_w], refs[n_w:2 * n_w], refs[2 * n_w]
    d = x_ref.shape[1]
    h = _modulated_norm(x_ref, mod_ref, g2_ref, 3, 4)
    ff = jnp.maximum(_matmul_w(h, wi_refs), 0.0)
    y_ref[...] = x_ref[...] + mod_ref[:, 5 * d:6 * d] * _matmul_w(ff * ff, wo_refs)


def _merge_call(oa, ob, oc, od, gates, x2d, mod, wb, wo, *, seq, tm):
    t, d = x2d.shape
    row = lambda n: pl.BlockSpec((tm, n), lambda i: (i, 0))
    weights = list(wb) + list(wo)
    return pl.pallas_call(
        _merge_kernel,
        grid=(t // tm,),
        in_specs=[row(BRANCH_WIDTH)] * 4 + [row(gates.shape[1]), row(d), _mod_spec(mod, seq, tm)]
        + [_const_spec(a.shape) for a in weights],
        out_specs=row(d),
        out_shape=jax.ShapeDtypeStruct((t, d), _F32),
        compiler_params=_params("parallel"),
        name="merge",
    )(oa, ob, oc, od, gates, x2d, mod, *weights)


def _mlp_call(x2d, mod, g2, wi, wo, *, seq, tm):
    t, d = x2d.shape
    row = lambda n: pl.BlockSpec((tm, n), lambda i: (i, 0))
    weights = list(wi) + list(wo)
    return pl.pallas_call(
        _mlp_kernel,
        grid=(t // tm,),
        in_specs=[row(d), _mod_spec(mod, seq, tm), _const_spec(g2.shape)] + [_const_spec(a.shape) for a in weights],
        out_specs=row(d),
        out_shape=jax.ShapeDtypeStruct((t, d), _F32),
        compiler_params=_params("parallel"),
        name="mlp",
    )(x2d, mod, g2, *weights)


PAGES_PER_STEP = 8
SELECT_PAGES_PER_STEP = 16


def _head_rows(row_vec, rows=8):
    shape = (rows, row_vec.shape[1])
    head_of_lane = lax.shift_right_logical(lax.broadcasted_iota(jnp.int32, shape, 1), 6)
    return jnp.where(head_of_lane == lax.broadcasted_iota(jnp.int32, shape, 0), row_vec, 0.0)


def _page_spec(layer, rows, slot_fn):
    return pl.BlockSpec((None, None, rows, LANES), lambda *a: (layer,) + slot_fn(*a))


def _moba_select_kernel(pt_ref, q_ref, *refs, n_blocks):
    pages = refs[:SELECT_PAGES_PER_STEP]
    ones_ref, o_ref, km_ref = refs[SELECT_PAGES_PER_STEP:]
    g = pl.program_id(1)
    blocks_per_step = SELECT_PAGES_PER_STEP // 2

    @pl.when(g == 0)
    def _():
        km_ref[...] = jnp.zeros(km_ref.shape, _F32)

    lane = lax.broadcasted_iota(jnp.int32, km_ref.shape, 1)
    km = km_ref[...]
    for i in range(blocks_per_step):
        both = pages[2 * i][...] + pages[2 * i + 1][...]
        sums = _dot_x_exact(both, ones_ref[...])
        km = jnp.where(lane == g * blocks_per_step + i, sums * (1.0 / MOBA_BLOCK), km)
    km_ref[...] = km

    @pl.when(g == pl.num_programs(1) - 1)
    def _():
        gate = jnp.dot(_head_rows(q_ref[...]), km, precision=_HIGHEST, preferred_element_type=_F32)
        lane8 = lax.broadcasted_iota(jnp.int32, gate.shape, 1)
        valid = lane8 < n_blocks
        gv = jnp.where(valid, gate, NEG_INF)
        cnt = jnp.zeros(gate.shape, _F32)
        for jp in range(n_blocks):
            col = gv[:, jp:jp + 1]
            cnt = cnt + ((col > gv) | ((col == gv) & (jp < lane8))).astype(_F32)
        sel = (valid & (cnt < MOBA_TOPK)).astype(_F32)
        pos = jnp.zeros(gate.shape, _F32)
        for jp in range(n_blocks):
            pos = pos + jnp.where(lane8 > jp, sel[:, jp:jp + 1], 0.0)
        out = jnp.zeros(gate.shape, jnp.int32)
        for slot in range(MOBA_TOPK):
            hit = (sel > 0.5) & (pos == float(slot))
            idx = jnp.sum(jnp.where(hit, lane8.astype(_F32), 0.0), axis=1, keepdims=True)
            out = jnp.where(lane8 == slot, idx.astype(jnp.int32), out)
        o_ref[...] = out


def _moba_select_call(layer, q3, cache_kt, page_table, ones):
    bs = q3.shape[0]
    n_pages = page_table.shape[1]
    per_step = SELECT_PAGES_PER_STEP
    assert n_pages % per_step == 0
    n_blocks = n_pages // 2
    assert MOBA_TOPK <= n_blocks <= LANES
    w = q3.shape[-1]
    page = lambda i: _page_spec(layer, w, lambda b, g, pt: (pt[b, g * per_step + i], 0, 0))
    return pl.pallas_call(
        functools.partial(_moba_select_kernel, n_blocks=n_blocks),
        grid_spec=pltpu.PrefetchScalarGridSpec(
            num_scalar_prefetch=1,
            grid=(bs, n_pages // per_step),
            in_specs=[pl.BlockSpec((None, 1, w), lambda b, g, pt: (b, 0, 0))]
            + [page(i) for i in range(per_step)] + [pl.BlockSpec(ones.shape, lambda b, g, pt: (0, 0))],
            out_specs=pl.BlockSpec((None, 8, LANES), lambda b, g, pt: (b, 0, 0)),
            scratch_shapes=[pltpu.VMEM((w, LANES), _F32)]),
        out_shape=jax.ShapeDtypeStruct((bs, 8, LANES), jnp.int32),
        compiler_params=_params("parallel", "arbitrary"),
        name="moba_select",
    )(page_table, q3, *([cache_kt] * per_step), ones)


def _moba_decode_kernel(pt_ref, sel_ref, q_ref, kn_ref, vn_ref, slope_ref, *refs, past_len):
    n_pg = 2 * MOBA_TOPK
    k_pages, v_pages, o_ref = refs[:n_pg], refs[n_pg:2 * n_pg], refs[2 * n_pg]
    b, h = pl.program_id(0), pl.program_id(1)
    qs = q_ref[...] * (HEAD_DIM ** -0.5)
    qs8 = jnp.broadcast_to(qs, (8, HEAD_DIM))
    slope = slope_ref[...]
    lane = lax.broadcasted_iota(jnp.int32, (1, LANES), 1)
    s_own = jnp.sum(qs * kn_ref[...], axis=1, keepdims=True)
    scores = []
    for i in range(n_pg):
        blk = sel_ref[b, h * MOBA_TOPK + i // 2]
        pos = blk * MOBA_BLOCK + (i % 2) * LANES + lane
        s = _dot_3pass(qs8, k_pages[i][...])[0:1, :]
        scores.append(s - slope * (past_len - pos).astype(_F32))
    m = s_own
    for s in scores:
        m = jnp.maximum(m, jnp.max(s, axis=1, keepdims=True))
    p_own = jnp.exp(s_own - m)
    l = p_own
    acc = p_own * vn_ref[...]
    for s, v_ref in zip(scores, v_pages):
        p = jnp.exp(s - m)
        l = l + jnp.sum(p, axis=1, keepdims=True)
        acc = acc + _dot_3pass(jnp.broadcast_to(p, (8, LANES)), v_ref[...], _NT)[0:1, :]
    o_ref[...] = acc / l


def _moba_decode_call(layer, q4, kn4, vn4, sel, cache_kt, cache_vt, page_table, slopes, past_len):
    bs, nh = q4.shape[:2]
    vec = pl.BlockSpec((None, None, 1, HEAD_DIM), lambda b, h, pt, sl: (b, h, 0, 0))

    def page(i):
        return _page_spec(layer, HEAD_DIM,
                          lambda b, h, pt, sl: (pt[b, 2 * sl[b, h * MOBA_TOPK + i // 2] + i % 2], h, 0))

    pages = [page(i) for i in range(2 * MOBA_TOPK)]
    return pl.pallas_call(
        functools.partial(_moba_decode_kernel, past_len=past_len),
        grid_spec=pltpu.PrefetchScalarGridSpec(
            num_scalar_prefetch=2,
            grid=(bs, nh),
            in_specs=[vec, vec, vec, pl.BlockSpec((None, 1, LANES), lambda b, h, pt, sl: (h, 0, 0))] + pages + pages,
            out_specs=vec),
        out_shape=jax.ShapeDtypeStruct(q4.shape, _F32),
        compiler_params=_params("parallel", "parallel"),
        name="moba_decode",
    )(page_table, sel, q4, kn4, vn4, slopes, *([cache_kt] * len(pages)), *([cache_vt] * len(pages)))


def _fox_decode_kernel(pt_ref, q_ref, kn_ref, vn_ref, smn_ref, *refs):
    n = PAGES_PER_STEP
    k_pages, v_pages, f_pages = refs[:n], refs[n:2 * n], refs[2 * n:3 * n]
    suf_ref, o_ref, qb_ref, m_ref, l_ref, acc_ref, carry_ref = refs[3 * n:]
    gi = pl.program_id(1)
    heads = [slice(h * HEAD_DIM, (h + 1) * HEAD_DIM) for h in range(N_HEADS)]

    @pl.when(gi == 0)
    def _():
        qcol = q_ref[...] * (HEAD_DIM ** -0.5)
        qb_ref[...] = jnp.broadcast_to(qcol, qb_ref.shape)
        prod = qcol * kn_ref[...]
        first = lax.broadcasted_iota(jnp.int32, (1, LANES), 1) == 0
        m_ref[...] = jnp.full(m_ref.shape, NEG_INF, _F32)
        l_ref[...] = jnp.zeros(l_ref.shape, _F32)
        for h, hs in enumerate(heads):
            m_ref[h:h + 1, :] = jnp.where(first, jnp.sum(prod[hs, :], axis=0, keepdims=True), NEG_INF)
            l_ref[h:h + 1, :] = first.astype(_F32)
            acc_ref[hs, :] = jnp.where(first, vn_ref[hs, :], 0.0)
        row8 = lax.broadcasted_iota(jnp.int32, (8, LANES), 0)
        lane8 = lax.broadcasted_iota(jnp.int32, (8, LANES), 1)
        carry_ref[...] = jnp.sum(jnp.where((lane8 == row8) & (row8 < N_HEADS), smn_ref[...], 0.0), axis=1, keepdims=True)

    suffix = carry_ref[...]
    lfs = [f_pages[i][...] for i in range(n)]
    in_page = [_dot_x_exact(lf, suf_ref[...]) for lf in lfs]
    bias = [None] * n
    for i in reversed(range(n)):
        bias[i] = in_page[i] + suffix
        suffix = suffix + (in_page[i][:, 0:1] + lfs[i][:, 0:1])
    carry_ref[...] = suffix
    for h, hs in enumerate(heads):
        qb = qb_ref[hs, :]
        m, l, acc = m_ref[h:h + 1, :], l_ref[h:h + 1, :], acc_ref[hs, :]
        for i in range(n):
            s = jnp.sum(qb * k_pages[i][hs, :], axis=0, keepdims=True) + bias[i][h:h + 1, :]
            m_new = jnp.maximum(m, s)
            alpha = jnp.exp(m - m_new)
            p = jnp.exp(s - m_new)
            l = alpha * l + p
            acc = alpha * acc + p * v_pages[i][hs, :]
            m = m_new
        m_ref[h:h + 1, :], l_ref[h:h + 1, :], acc_ref[hs, :] = m, l, acc

    @pl.when(gi == pl.num_programs(1) - 1)
    def _():
        for h, hs in enumerate(heads):
            m = m_ref[h:h + 1, :]
            wgt = jnp.exp(m - jnp.max(m, axis=1, keepdims=True))
            total = jnp.sum(l_ref[h:h + 1, :] * wgt, axis=1, keepdims=True)
            o_ref[hs, :] = jnp.sum(acc_ref[hs, :] * wgt, axis=1, keepdims=True) / total


def _fox_decode_call(layer, qc, knc, vnc, smn3, cache_kt, cache_vt, cache_lf, page_table, suf):
    bs, w, _ = qc.shape
    n_pages = page_table.shape[1]
    n_groups = n_pages // PAGES_PER_STEP
    assert n_pages % PAGES_PER_STEP == 0

    def page(i, rows):
        return _page_spec(layer, rows, lambda b, g, pt: (pt[b, (n_groups - 1 - g) * PAGES_PER_STEP + i], 0, 0))

    col = pl.BlockSpec((None, w, 1), lambda b, g, pt: (b, 0, 0))
    rng = range(PAGES_PER_STEP)
    return pl.pallas_call(
        _fox_decode_kernel,
        grid_spec=pltpu.PrefetchScalarGridSpec(
            num_scalar_prefetch=1,
            grid=(bs, n_groups),
            in_specs=[col, col, col, pl.BlockSpec((None, 1, SMALL_COLS), lambda b, g, pt: (b, 0, 0))]
            + [page(i, w) for i in rng] + [page(i, w) for i in rng] + [page(i, 8) for i in rng]
            + [pl.BlockSpec(suf.shape, lambda b, g, pt: (0, 0))],
            out_specs=col,
            scratch_shapes=[pltpu.VMEM((w, LANES), _F32), pltpu.VMEM((8, LANES), _F32), pltpu.VMEM((8, LANES), _F32),
                            pltpu.VMEM((w, LANES), _F32), pltpu.VMEM((8, 1), _F32)]),
        out_shape=jax.ShapeDtypeStruct((bs, w, 1), _F32),
        compiler_params=_params("parallel", "arbitrary"),
        name="fox_decode",
    )(page_table, qc, knc, vnc, smn3, *([cache_kt] * PAGES_PER_STEP), *([cache_vt] * PAGES_PER_STEP),
      *([cache_lf] * PAGES_PER_STEP), suf)


def _gdn_decode_kernel(x_ref, buf_ref, z_ref, sm_ref, cw_ref, on_ref, s0_ref, o_ref, s_out_ref):
    w = BRANCH_WIDTH
    conv = cw_ref[GDN_CONV - 1:GDN_CONV, :] * x_ref[...]
    for j in range(GDN_CONV - 1):
        conv = conv + cw_ref[j:j + 1, :] * buf_ref[j:j + 1, :]
    act = conv * jax.nn.sigmoid(conv)
    sm = sm_ref[...]
    z = z_ref[...]
    eye = lax.broadcasted_iota(jnp.int32, (HEAD_DIM, HEAD_DIM), 0) == lax.broadcasted_iota(jnp.int32, (HEAD_DIM, HEAD_DIM), 1)
    to_col = lambda r: jnp.sum(jnp.where(eye, r, 0.0), axis=1, keepdims=True)
    l2 = lambda a: a * lax.rsqrt(jnp.sum(a * a, axis=1, keepdims=True) + EPS)
    for h in range(N_HEADS):
        hs = slice(h * HEAD_DIM, (h + 1) * HEAD_DIM)
        qh = l2(act[:, hs]) * (HEAD_DIM ** -0.5)
        kh = l2(act[:, w + h * HEAD_DIM:w + (h + 1) * HEAD_DIM])
        vh = act[:, 2 * w + h * HEAD_DIM:2 * w + (h + 1) * HEAD_DIM]
        beta = sm[:, N_HEADS + h:N_HEADS + h + 1]
        eg = jnp.exp(sm[:, 2 * N_HEADS + h:2 * N_HEADS + h + 1])
        state = s0_ref[h]
        kcol, qcol = to_col(kh), to_col(qh)
        v_new = beta * (vh - eg * jnp.sum(kcol * state, axis=0, keepdims=True))
        o = eg * jnp.sum(qcol * state, axis=0, keepdims=True) + jnp.sum(qh * kh, axis=1, keepdims=True) * v_new
        s_out_ref[h] = state * eg + kcol * v_new
        on = o * lax.rsqrt(jnp.mean(o * o, axis=-1, keepdims=True) + EPS) * on_ref[...]
        zh = z[:, hs]
        o_ref[:, hs] = on * (zh * jax.nn.sigmoid(zh))


def _gdn_decode_call(x3, buf, z3, sm3, conv_w, on_gain, s0):
    bs = x3.shape[0]
    w = BRANCH_WIDTH
    vec = lambda n: pl.BlockSpec((None, 1, n), lambda b: (b, 0, 0))
    st = pl.BlockSpec((None, N_HEADS, HEAD_DIM, HEAD_DIM), lambda b: (b, 0, 0, 0))
    return pl.pallas_call(
        _gdn_decode_kernel,
        grid=(bs,),
        in_specs=[vec(3 * w), pl.BlockSpec((None, GDN_CONV - 1, 3 * w), lambda b: (b, 0, 0)), vec(w), vec(SMALL_COLS),
                  _const_spec(conv_w.shape), _const_spec(on_gain.shape), st],
        out_specs=[vec(w), st],
        out_shape=[jax.ShapeDtypeStruct((bs, 1, w), _F32), jax.ShapeDtypeStruct(s0.shape, _F32)],
        compiler_params=_params("parallel"),
        name="gdn_decode",
    )(x3, buf, z3, sm3, conv_w, on_gain, s0)


def _constants():
    w = BRANCH_WIDTH
    i = jnp.arange(w)
    c = jnp.arange(GDN_CHUNK)
    return dict(
        seg=(i[:, None] // HEAD_DIM == i[None, :] // HEAD_DIM).astype(_BF16),
        tri_blk=(i[:, None] >= i[None, :]).astype(_BF16),
        tri_l=(c[:, None] >= c[None, :]).astype(_BF16),
        ones=jnp.ones((LANES, LANES), _BF16),
        suf=(jnp.arange(LANES)[:, None] > jnp.arange(LANES)[None, :]).astype(_BF16),
        slopes=jnp.broadcast_to(jnp.asarray([_alibi_slope(h) for h in range(N_HEADS)], _F32)[:, None, None],
                                (N_HEADS, 1, LANES)),
    )


def _block_diag(t):
    l, g, a, b = t.shape
    eye = jnp.eye(g, dtype=bool)
    out = jnp.where(eye[None, :, None, :, None], t[:, :, :, None, :], jnp.zeros((), t.dtype))
    return out.reshape(l, g * a, g * b)


def _prep_weights(p):
    depth, d, _ = p["w_in"].shape
    w = BRANCH_WIDTH
    nh = N_HEADS
    w_in = p["w_in"]
    off_fb = 6 * w
    off_s5 = off_fb + nh
    off_beta = off_s5 + w + 4 * w
    off_gate = off_beta + 2 * nh
    wm = jnp.concatenate([w_in[:, :, 0:off_fb], w_in[:, :, off_s5:off_beta], w_in[:, :, off_gate:]], axis=-1)
    ws = jnp.concatenate([w_in[:, :, off_fb:off_s5], w_in[:, :, off_beta:off_gate],
                          jnp.zeros((depth, d, SMALL_COLS - 3 * nh), _F32)], axis=-1)
    tile_h = lambda g: jnp.tile(g, (1, nh))
    zeros_w = jnp.zeros((depth, w), _F32)
    gains = jnp.stack([tile_h(p["moba_qn_g"]), tile_h(p["moba_kn_g"]), tile_h(p["fox_qn_g"]), tile_h(p["fox_kn_g"]),
                       zeros_w, zeros_w, zeros_w, zeros_w], axis=1)
    z4 = jnp.zeros((depth, nh), _F32)
    pad = jnp.zeros((depth, SMALL_COLS - 3 * nh), _F32)
    sp_bias = jnp.concatenate([p["fox_f_bias"], z4, p["gdn_dt_bias"], pad], axis=-1)
    sp_alog = jnp.concatenate([z4, z4, p["gdn_a_log"], pad], axis=-1)
    sp = jnp.concatenate([sp_bias[:, None], sp_alog[:, None], jnp.zeros((depth, 6, SMALL_COLS), _F32)], axis=1)
    wt = wm[:, :, 0:6 * w].transpose(0, 2, 1)
    wst = ws[:, :, 0:16].transpose(0, 2, 1)
    gains_t = gains.transpose(0, 2, 1)
    sp_t = jnp.concatenate([sp[:, 0:2, 0:16].transpose(0, 2, 1), jnp.zeros((depth, 16, 6), _F32)], axis=-1)

    g, s = p["s5_lambda_re"].shape[1:]
    flat = lambda a: a.reshape(depth * g, -1)
    ldt = jnp.broadcast_to(p["s5_log_dt"][:, :, None], (depth, g, s))
    lbre, lbim, fre, fim = _whole_call(_s5_disc_kernel, 4, (depth * g, s), "s5_disc",
                                       flat(p["s5_lambda_re"]), flat(p["s5_lambda_im"]), flat(ldt))
    rep = lambda a: jnp.repeat(a, S5_GROUP, axis=-1)
    bbre, bbim = _whole_call(_s5_bbar_kernel, 2, (depth * g, s * S5_GROUP), "s5_bbar",
                             rep(fre), rep(fim), flat(p["s5_b_re"]), flat(p["s5_b_im"]))
    to_hp = lambda a: a.reshape(depth, g, s, S5_GROUP).transpose(0, 1, 3, 2)
    bbd = jnp.concatenate([_block_diag(to_hp(bbre)), _block_diag(to_hp(bbim))], axis=-1)
    to_ph = lambda a: a.transpose(0, 1, 3, 2)
    cre = _block_diag(to_ph(p["s5_c_re"]))
    cim = _block_diag(to_ph(p["s5_c_im"]))

    conv_w = jnp.concatenate([p["gdn_conv_w"], jnp.zeros((depth, 8 - GDN_CONV, 3 * w), _F32)], axis=1)
    out = dict(
        g1=p["norm1_g"][:, None, :], g2=p["norm2_g"][:, None, :], gains=gains, sp=sp,
        wt=wt.astype(_BF16), wst=wst.astype(_BF16), gains_t=gains_t, sp_t=sp_t,
        lbre=lbre.reshape(depth, 1, g * s), lbim=lbim.reshape(depth, 1, g * s),
        dskip=p["s5_d"][:, None, :], conv_w=conv_w, on_gain=p["gdn_out_g"][:, None, :],
    )
    for name, a in dict(wm=wm, ws=ws, bbd=bbd, cre=cre, cim=cim, glu=p["s5_w_glu"], wb=p["w_branch"], wo=p["w_out"],
                        wi=p["mlp_in"], wo2=p["mlp_out"]).items():
        out[name], out[name + "_lo"] = _split2(a)
    return out


def _layer_prompt(x2d, mod, lw, cst, batch, seq):
    t = batch * seq
    w = BRANCH_WIDTH
    tm = 256
    r3 = lambda a: a.reshape(batch, seq, a.shape[-1])
    qat, kat, vat, qbt, kbt, vbt, kar, kmean, kbaug, u, qkvd, zd, gates, sm, smt = _inproj_prompt_call(
        x2d, mod, lw, cst["seg"], cst["tri_blk"], batch=batch, seq=seq)
    nb = kmean.shape[1]
    kmean = jnp.pad(kmean.reshape(batch, nb, w), ((0, 0), (0, -nb % 8), (0, 0)))
    oa = _moba_prompt_call(qat, kar, vat, kmean)
    ob = _fox_prompt_call(qbt, kbaug, vbt)
    n = lw["lbre"].shape[-1]
    u_tm = r3(u).transpose(1, 0, 2).reshape(t, w)
    h0 = jnp.zeros((batch, n), _F32)
    oc_tm, sre, sim = _s5_call(u_tm, h0, h0, lw["lbre"], lw["lbim"], lw["dskip"], (lw["bbd"],), (lw["cre"],),
                               (lw["cim"],), (lw["glu"],), steps=64)
    oc = oc_tm.reshape(seq, batch, w).transpose(1, 0, 2).reshape(t, w)
    od, gstate = _gdn_prompt_call(r3(qkvd), r3(zd), r3(sm), lw["conv_w"], lw["on_gain"], cst["seg"], cst["tri_l"])
    x1 = _merge_call(oa.reshape(t, w), ob.reshape(t, w), oc, od.reshape(t, w), gates, x2d, mod, (lw["wb"],),
                     (lw["wo"],), seq=seq, tm=tm)
    x2 = _mlp_call(x1, mod, lw["g2"], (lw["wi"],), (lw["wo2"],), seq=seq, tm=tm)
    groups = n // S5_STATE
    new = dict(moba_k=kat, moba_v=vat, fox_k=kbt, fox_v=vbt, fox_logf=smt[:, 0:N_HEADS, :],
               s5_re=sre.reshape(batch, groups, S5_STATE), s5_im=sim.reshape(batch, groups, S5_STATE),
               gdn=gstate, gdn_conv=r3(qkvd)[:, seq - (GDN_CONV - 1):, :])
    return x2, new


def _layer_sample(x2d, mod, lw, cst, layer, caches, page_table, past):
    bs = x2d.shape[0]
    w = BRANCH_WIDTH
    nh = N_HEADS
    past_len = page_table.shape[1] * LANES
    qa, ka, va, qb, kb, vb, u, qkvd, zd, gates, sm = _inproj_sample_call(x2d, mod, lw, cst["seg"])
    r3 = lambda a: a.reshape(bs, 1, a.shape[-1])
    r4 = lambda a: a.reshape(bs, nh, 1, HEAD_DIM)
    picked = _moba_select_call(layer, r3(qa), caches["moba_k"], page_table, cst["ones"])
    sel = picked[:, 0:nh, 0:MOBA_TOPK].reshape(bs, nh * MOBA_TOPK)
    oa = _moba_decode_call(layer, r4(qa), r4(ka), r4(va), sel, caches["moba_k"], caches["moba_v"], page_table,
                           cst["slopes"], past_len)
    col = lambda a: a.reshape(bs, w, 1)
    ob = _fox_decode_call(layer, col(qb), col(kb), col(vb), r3(sm), caches["fox_k"], caches["fox_v"],
                          caches["fox_logf"], page_table, cst["suf"])
    n = lw["lbre"].shape[-1]
    pair = lambda name: (lw[name], lw[name + "_lo"])
    oc, sre, sim = _s5_call(u, past["s5_re"].reshape(bs, n), past["s5_im"].reshape(bs, n), lw["lbre"], lw["lbim"],
                            lw["dskip"], pair("bbd"), pair("cre"), pair("cim"), pair("glu"), steps=1)
    od, gstate = _gdn_decode_call(r3(qkvd), past["gdn_conv"], r3(zd), r3(sm), lw["conv_w"], lw["on_gain"], past["gdn"])
    x1 = _merge_call(oa.reshape(bs, w), ob.reshape(bs, w), oc, od.reshape(bs, w), gates, x2d, mod, pair("wb"),
                     pair("wo"), seq=1, tm=bs)
    x2 = _mlp_call(x1, mod, lw["g2"], pair("wi"), pair("wo2"), seq=1, tm=bs)
    heads = lambda a: a.reshape(bs, 1, nh, HEAD_DIM)
    new = dict(moba_k=heads(ka), moba_v=heads(va), fox_k=heads(kb), fox_v=heads(vb),
               fox_logf=sm[:, 0:nh].reshape(bs, 1, nh),
               s5_re=sre.reshape(past["s5_re"].shape), s5_im=sim.reshape(past["s5_im"].shape), gdn=gstate,
               gdn_conv=jnp.concatenate([past["gdn_conv"][:, 1:], r3(qkvd)], axis=1))
    return x2, new


_STATE_KEYS = ("moba_k", "moba_v", "fox_k", "fox_v", "fox_logf", "s5_re", "s5_im", "gdn", "gdn_conv")


def kernel(x_prompt, x_sample, cache_moba_k, cache_moba_v, cache_fox_k, cache_fox_v, cache_fox_logf, state_s5_re, state_s5_im, state_gdn, state_gdn_conv, page_table, c_prompt, c_sample, norm1_g, norm2_g, ada_w, ada_b, w_in, moba_qn_g, moba_kn_g, fox_qn_g, fox_kn_g, fox_f_bias, s5_lambda_re, s5_lambda_im, s5_b_re, s5_b_im, s5_c_re, s5_c_im, s5_d, s5_log_dt, s5_w_glu, gdn_conv_w, gdn_a_log, gdn_dt_bias, gdn_out_g, w_branch, w_out, mlp_in, mlp_out):
    batch, seq, d = x_prompt.shape
    bs, dec_seq, _ = x_sample.shape
    depth = w_in.shape[0]
    assert dec_seq == 1 and page_table.shape == (bs, page_table.shape[1]) and cache_moba_k.shape[2] == LANES
    assert (batch + bs) % 8 == 0 and batch % 8 == 0 and bs % 8 == 0
    weights = dict(norm1_g=norm1_g, norm2_g=norm2_g, w_in=w_in, moba_qn_g=moba_qn_g, moba_kn_g=moba_kn_g,
                   fox_qn_g=fox_qn_g, fox_kn_g=fox_kn_g, fox_f_bias=fox_f_bias, s5_lambda_re=s5_lambda_re,
                   s5_lambda_im=s5_lambda_im, s5_b_re=s5_b_re, s5_b_im=s5_b_im, s5_c_re=s5_c_re, s5_c_im=s5_c_im,
                   s5_d=s5_d, s5_log_dt=s5_log_dt, s5_w_glu=s5_w_glu, gdn_conv_w=gdn_conv_w, gdn_a_log=gdn_a_log,
                   gdn_dt_bias=gdn_dt_bias, gdn_out_g=gdn_out_g, w_branch=w_branch, w_out=w_out, mlp_in=mlp_in,
                   mlp_out=mlp_out)
    lw_all = _prep_weights(weights)
    cst = _constants()
    mod_all = _ada_call(jnp.concatenate([c_prompt, c_sample], axis=0), ada_w, ada_b)

    pages_t = lambda c: c.transpose(0, 1, 3, 4, 2).reshape(c.shape[0], c.shape[1], BRANCH_WIDTH, LANES)
    logf_t = jnp.pad(cache_fox_logf.transpose(0, 1, 3, 2), ((0, 0), (0, 0), (0, 8 - N_HEADS), (0, 0)))
    caches = dict(moba_k=pages_t(cache_moba_k), moba_v=pages_t(cache_moba_v), fox_k=pages_t(cache_fox_k),
                  fox_v=pages_t(cache_fox_v), fox_logf=logf_t)

    xp = x_prompt.reshape(batch * seq, d)
    xs = x_sample.reshape(bs, d)
    new_p = {k: [] for k in _STATE_KEYS}
    new_s = {k: [] for k in _STATE_KEYS}
    for layer in range(depth):
        lw = {k: v[layer] for k, v in lw_all.items()}
        xp, st_p = _layer_prompt(xp, mod_all[layer, 0:batch][:, None, :], lw, cst, batch, seq)
        past = dict(s5_re=state_s5_re[layer], s5_im=state_s5_im[layer], gdn=state_gdn[layer],
                    gdn_conv=state_gdn_conv[layer])
        xs, st_s = _layer_sample(xs, mod_all[layer, batch:], lw, cst, layer, caches, page_table, past)
        for k in _STATE_KEYS:
            new_p[k].append(st_p[k])
            new_s[k].append(st_s[k])
    pn = {k: jnp.stack(v) for k, v in new_p.items()}
    sn = {k: jnp.stack(v) for k, v in new_s.items()}
    heads_last = lambda a: a.reshape(depth, batch, N_HEADS, HEAD_DIM, seq).transpose(0, 1, 4, 2, 3)
    return (xp.reshape(batch, seq, d), xs.reshape(bs, 1, d),
            heads_last(pn["moba_k"]), heads_last(pn["moba_v"]), heads_last(pn["fox_k"]), heads_last(pn["fox_v"]),
            pn["fox_logf"].transpose(0, 1, 3, 2), pn["s5_re"], pn["s5_im"], pn["gdn"], pn["gdn_conv"],
            sn["moba_k"], sn["moba_v"], sn["fox_k"], sn["fox_v"], sn["fox_logf"],
            sn["s5_re"], sn["s5_im"], sn["gdn"], sn["gdn_conv"])
```

```python
import functools

import jax
import jax.numpy as jnp
from jax import lax
from jax.experimental import pallas as pl
from jax.experimental.pallas import tpu as pltpu

HEAD_DIM = 64
N_BRANCH = 4
N_HEADS = 4
BRANCH_WIDTH = N_HEADS * HEAD_DIM
MOBA_BLOCK = 256
MOBA_TOPK = 3
S5_GROUP = 16
S5_STATE = 64
GDN_CONV = 4
GDN_CHUNK = 64
N_ADA = 6
EPS = 1e-6
NEG_INF = -1e30
LANES = 128
SMALL_COLS = 128
VMEM_LIMIT_BYTES = 56 * 1024 * 1024

_F32 = jnp.float32
_BF16 = jnp.bfloat16
_HIGHEST = lax.Precision.HIGHEST
_NT = (((1,), (1,)), ((), ()))


def _params(*semantics):
    return pltpu.CompilerParams(dimension_semantics=semantics, vmem_limit_bytes=VMEM_LIMIT_BYTES)


def _dot(a, b):
    return jnp.dot(a.astype(_BF16), b.astype(_BF16), preferred_element_type=_F32)


def _dot_nt(a, b):
    return lax.dot_general(a.astype(_BF16), b.astype(_BF16), _NT, preferred_element_type=_F32)


def _split3(x):
    hi = x.astype(_BF16)
    r1 = x - hi.astype(_F32)
    mid = r1.astype(_BF16)
    lo = (r1 - mid.astype(_F32)).astype(_BF16)
    return hi, mid, lo


def _dot_x_exact(x, m):
    hi, mid, lo = _split3(x)
    d = lambda a: jnp.dot(a, m, preferred_element_type=_F32)
    return d(hi) + d(mid) + d(lo)


def _exact_dot_x(m, x):
    hi, mid, lo = _split3(x)
    d = lambda a: jnp.dot(m, a, preferred_element_type=_F32)
    return d(hi) + d(mid) + d(lo)


def _const_spec(shape):
    nd = len(shape)
    return pl.BlockSpec(shape, lambda *_: (0,) * nd)


def _ada_kernel(c_ref, w_ref, b_ref, o_ref):
    c = c_ref[...]
    o_ref[...] = _dot_3pass(c * jax.nn.sigmoid(c), w_ref[...]) + b_ref[...]


def _ada_call(c_all, ada_w, ada_b):
    depth, d, n = ada_w.shape
    rows = c_all.shape[0]
    tn = n // 4
    return pl.pallas_call(
        _ada_kernel,
        grid=(depth, n // tn),
        in_specs=[pl.BlockSpec((rows, d), lambda l, j: (0, 0)),
                  pl.BlockSpec((None, d, tn), lambda l, j: (l, 0, j)),
                  pl.BlockSpec((None, 1, tn), lambda l, j: (l, 0, j))],
        out_specs=pl.BlockSpec((None, rows, tn), lambda l, j: (l, 0, j)),
        out_shape=jax.ShapeDtypeStruct((depth, rows, n), _F32),
        compiler_params=_params("parallel", "parallel"),
        name="ada_mod",
    )(c_all, ada_w, ada_b.reshape(depth, 1, n))


def _scalar_gates(v_raw, bias, alog, index):
    v = v_raw + bias
    logf = jax.nn.log_sigmoid(v)
    beta = jax.nn.sigmoid(v_raw)
    g = -jnp.exp(alog) * jax.nn.softplus(v)
    nh = N_HEADS
    return jnp.where(index < nh, logf, jnp.where(index < 2 * nh, beta, jnp.where(index < 3 * nh, g, 0.0)))


def _mod_spec(mod, seq, tm):
    if seq > 1:
        tiles_per_batch = seq // tm
        return pl.BlockSpec((None, 1, mod.shape[-1]), lambda i: (i // tiles_per_batch, 0, 0))
    return pl.BlockSpec((tm, mod.shape[-1]), lambda i: (i, 0))


def _modulated_norm(x_ref, mod_ref, g_ref, shift_chunk, scale_chunk):
    d = x_ref.shape[1]
    x = x_ref[...]
    y = x * lax.rsqrt(jnp.mean(x * x, axis=-1, keepdims=True) + EPS) * g_ref[...]
    return y * (1.0 + mod_ref[:, scale_chunk * d:(scale_chunk + 1) * d]) + mod_ref[:, shift_chunk * d:(shift_chunk + 1) * d]


def _head_norm_rows(p, seg_ref, gain_row):
    ss = _dot_x_exact(p * p, seg_ref[...])
    return p * lax.rsqrt(ss * (1.0 / HEAD_DIM) + EPS) * gain_row


def _inproj_common(slab, small, sp_ref, u_ref, qkvd_ref, zd_ref, sm_ref):
    w = BRANCH_WIDTH
    u_ref[...] = slab(6)
    for i in range(3):
        qkvd_ref[:, i * w:(i + 1) * w] = slab(7 + i)
    zd_ref[...] = slab(10)
    out = _scalar_gates(small, sp_ref[0:1, :], sp_ref[1:2, :], lax.broadcasted_iota(jnp.int32, small.shape, 1))
    sm_ref[...] = out
    return out


def _inproj_sample_kernel(x_ref, mod_ref, g1_ref, wm_ref, wml_ref, ws_ref, wsl_ref, gains_ref, sp_ref, seg_ref,
                          qa_ref, ka_ref, va_ref, qb_ref, kb_ref, vb_ref, u_ref, qkvd_ref, zd_ref, sm_ref):
    w = BRANCH_WIDTH
    h = _modulated_norm(x_ref, mod_ref, g1_ref, 0, 1)
    slab = lambda s: _dot_hi_lo(h, wm_ref[:, s * w:(s + 1) * w], wml_ref[:, s * w:(s + 1) * w])
    _inproj_common(slab, _dot_hi_lo(h, ws_ref[...], wsl_ref[...]), sp_ref, u_ref, qkvd_ref, zd_ref, sm_ref)
    qa_ref[...] = _head_norm_rows(slab(0), seg_ref, gains_ref[0:1, :])
    ka_ref[...] = _head_norm_rows(slab(1), seg_ref, gains_ref[1:2, :])
    va_ref[...] = slab(2)
    qb_ref[...] = _head_norm_rows(slab(3), seg_ref, gains_ref[2:3, :])
    kb_ref[...] = _head_norm_rows(slab(4), seg_ref, gains_ref[3:4, :])
    vb_ref[...] = slab(5)


def _inproj_sample_call(x2d, mod, lw, seg):
    t, d = x2d.shape
    w = BRANCH_WIDTH
    row = lambda n: pl.BlockSpec((t, n), lambda i: (i, 0))
    widths = [w] * 7 + [3 * w, w, SMALL_COLS]
    args = [x2d, mod, lw["g1"], lw["wm"], lw["wm_lo"], lw["ws"], lw["ws_lo"], lw["gains"], lw["sp"], seg]
    return pl.pallas_call(
        _inproj_sample_kernel,
        grid=(1,),
        in_specs=[row(d), _mod_spec(mod, 1, t)] + [_const_spec(a.shape) for a in args[2:]],
        out_specs=[row(n) for n in widths],
        out_shape=[jax.ShapeDtypeStruct((t, n), _F32) for n in widths],
        compiler_params=_params("arbitrary"),
        name="inproj_sample",
    )(*args)


FOX_BIAS_TERMS = 3


def _inproj_prompt_kernel(x_ref, mod_ref, g1_ref, wm_ref, ws_ref, gains_ref, sp_ref, seg_ref, wt_ref, wst_ref, gt_ref,
                          spt_ref, tri_ref,
                          qat_ref, kat_ref, vat_ref, qbt_ref, kbt_ref, vbt_ref, kar_ref, kmean_ref, kbaug_ref,
                          u_ref, qkvd_ref, zd_ref, sm_ref, smt_ref, cum_ref):
    w = BRANCH_WIDTH
    hb = _modulated_norm(x_ref, mod_ref, g1_ref, 0, 1).astype(_BF16)
    slab = lambda s: jnp.dot(hb, wm_ref[:, s * w:(s + 1) * w], preferred_element_type=_F32)
    sm = _inproj_common(slab, jnp.dot(hb, ws_ref[...], preferred_element_type=_F32), sp_ref,
                        u_ref, qkvd_ref, zd_ref, sm_ref)

    def slab_t(i):
        return lax.dot_general(wt_ref[i * w:(i + 1) * w, :], hb, _NT, preferred_element_type=_F32)

    def put_head_norm_t(ref, p, col):
        for hd in range(N_HEADS):
            hs = slice(hd * HEAD_DIM, (hd + 1) * HEAD_DIM)
            ph = p[hs, :]
            ms = jnp.mean(ph * ph, axis=0, keepdims=True)
            ref[hs, :] = ph * lax.rsqrt(ms + EPS) * gt_ref[hs, col:col + 1]

    put_head_norm_t(qat_ref, slab_t(0), 0)
    put_head_norm_t(kat_ref, slab_t(1), 1)
    vat_ref[...] = slab_t(2)
    put_head_norm_t(qbt_ref, slab_t(3), 2)
    put_head_norm_t(kbt_ref, slab_t(4), 3)
    vbt_ref[...] = slab_t(5)
    smt = lax.dot_general(wst_ref[...], hb, _NT, preferred_element_type=_F32)
    smt_ref[...] = _scalar_gates(smt, spt_ref[:, 0:1], spt_ref[:, 1:2], lax.broadcasted_iota(jnp.int32, smt.shape, 0))

    ka = _head_norm_rows(slab(1), seg_ref, gains_ref[1:2, :])
    kmean_ref[...] = jnp.mean(ka, axis=0, keepdims=True)
    for hd in range(N_HEADS):
        kar_ref[hd] = ka[:, hd * HEAD_DIM:(hd + 1) * HEAD_DIM].astype(_BF16)

    @pl.when(pl.program_id(1) == 0)
    def _():
        cum_ref[...] = jnp.zeros(cum_ref.shape, _F32)

    cum = _exact_dot_x(tri_ref[...], sm) + cum_ref[...]
    cum_ref[...] = cum[cum.shape[0] - 1:, :]
    kb = _head_norm_rows(slab(4), seg_ref, gains_ref[3:4, :])
    lane = lax.broadcasted_iota(jnp.int32, (kb.shape[0], LANES), 1)
    for hd in range(N_HEADS):
        pair = kb[:, (hd // 2) * LANES:(hd // 2 + 1) * LANES]
        if hd % 2:
            pair = pltpu.roll(pair, HEAD_DIM, axis=1)
        aug = jnp.where(lane < HEAD_DIM, pair, 0.0)
        for i, term in enumerate(_split3(-cum[:, hd:hd + 1])):
            aug = jnp.where(lane == HEAD_DIM + i, term.astype(_F32), aug)
        kbaug_ref[hd] = aug.astype(_BF16)


def _inproj_prompt_call(x2d, mod, lw, seg, tri, *, batch, seq):
    t, d = x2d.shape
    w = BRANCH_WIDTH
    tm = MOBA_BLOCK
    assert seq % tm == 0
    nb = seq // tm
    row = lambda n: pl.BlockSpec((tm, n), lambda b, j: (b * nb + j, 0))
    rs = lambda n: jax.ShapeDtypeStruct((t, n), _F32)
    t_spec = lambda r: pl.BlockSpec((None, r, tm), lambda b, j: (b, 0, j))
    t_shape = lambda r: jax.ShapeDtypeStruct((batch, r, seq), _F32)
    head_rows = lambda n: pl.BlockSpec((None, N_HEADS, tm, n), lambda b, j: (b, 0, j, 0))
    args = [x2d, mod, lw["g1"], lw["wm"], lw["ws"], lw["gains"], lw["sp"], seg, lw["wt"], lw["wst"], lw["gains_t"],
            lw["sp_t"], tri]
    out_specs = [t_spec(w)] * 6 + [head_rows(HEAD_DIM), pl.BlockSpec((None, None, 1, w), lambda b, j: (b, j, 0, 0)),
                                   head_rows(LANES), row(w), row(3 * w), row(w), row(SMALL_COLS), t_spec(16)]
    out_shape = [t_shape(w)] * 6 + [jax.ShapeDtypeStruct((batch, N_HEADS, seq, HEAD_DIM), _BF16),
                                    jax.ShapeDtypeStruct((batch, nb, 1, w), _F32),
                                    jax.ShapeDtypeStruct((batch, N_HEADS, seq, LANES), _BF16),
                                    rs(w), rs(3 * w), rs(w), rs(SMALL_COLS), t_shape(16)]
    return pl.pallas_call(
        _inproj_prompt_kernel,
        grid=(batch, nb),
        in_specs=[row(d), pl.BlockSpec((None, 1, mod.shape[-1]), lambda b, j: (b, 0, 0))]
        + [_const_spec(a.shape) for a in args[2:]],
        out_specs=out_specs,
        out_shape=out_shape,
        scratch_shapes=[pltpu.VMEM((1, SMALL_COLS), _F32)],
        compiler_params=_params("parallel", "arbitrary"),
        name="inproj_prompt",
    )(*args)


def _alibi_slope(h):
    return float(2.0 ** (-8.0 * (h + 1) / N_HEADS))


def _init_softmax_state(m_ref, l_ref, acc_ref):
    m_ref[...] = jnp.full(m_ref.shape, NEG_INF, _F32)
    l_ref[...] = jnp.zeros(l_ref.shape, _F32)
    acc_ref[...] = jnp.zeros(acc_ref.shape, _F32)


def _softmax_steps_t(scores, values, m_ref, l_ref, acc_ref):
    probs, alphas = [], []
    for h, s in enumerate(scores):
        m_old = m_ref[h]
        m_new = jnp.maximum(m_old, jnp.max(s, axis=0, keepdims=True))
        alpha = jnp.exp(m_old - m_new)
        p = jnp.exp(s - m_new)
        l_ref[h] = alpha * l_ref[h] + jnp.sum(p, axis=0, keepdims=True)
        m_ref[h] = m_new
        probs.append(p.astype(_BF16))
        alphas.append(alpha)
    pv = [jnp.dot(v.astype(_BF16), p, preferred_element_type=_F32) for v, p in zip(values, probs)]
    for h in range(len(scores)):
        acc_ref[h] = alphas[h] * acc_ref[h] + pv[h]


def _softmax_scratch(tq):
    return [pltpu.VMEM((N_HEADS, 1, tq), _F32), pltpu.VMEM((N_HEADS, 1, tq), _F32),
            pltpu.VMEM((N_HEADS, HEAD_DIM, tq), _F32)]


def _write_attention_out(o_ref, l_ref, acc_ref):
    out_t = jnp.concatenate([acc_ref[h] / l_ref[h] for h in range(N_HEADS)], axis=0)
    o_ref[...] = out_t.T


def _moba_prompt_kernel(qt_ref, kr_ref, vt_ref, km_ref, o_ref, sel_ref, m_ref, l_ref, acc_ref, *, nb):
    qi = pl.program_id(1)
    tq = qt_ref.shape[1]
    blk = MOBA_BLOCK
    nbp = km_ref.shape[0]
    blk_row = lax.broadcasted_iota(jnp.int32, (nbp, tq), 0)
    rc = (lax.broadcasted_iota(jnp.int32, (blk, tq), 1) - lax.broadcasted_iota(jnp.int32, (blk, tq), 0)).astype(_F32)
    qs = []
    for h in range(N_HEADS):
        hs = slice(h * HEAD_DIM, (h + 1) * HEAD_DIM)
        qt = qt_ref[hs, :]
        gate = jnp.dot(km_ref[:, hs], qt, precision=_HIGHEST, preferred_element_type=_F32)
        valid = blk_row < qi
        gv = jnp.where(valid, gate, NEG_INF)
        cnt = jnp.zeros((nbp, tq), _F32)
        for jp in range(nb):
            other = gv[jp:jp + 1, :]
            cnt = cnt + ((other > gv) | ((other == gv) & (jp < blk_row))).astype(_F32)
        sel_ref[h] = (valid & (cnt < MOBA_TOPK)).astype(_F32)
        qs.append((qt * (HEAD_DIM ** -0.5)).astype(_BF16))
    _init_softmax_state(m_ref, l_ref, acc_ref)

    def block(j, own):
        start = pl.multiple_of(j * blk, blk)
        dist = rc + ((qi - j) * blk).astype(_F32)
        raw = [jnp.dot(kr_ref[h, pl.ds(start, blk), :], qs[h], preferred_element_type=_F32)
               for h in range(N_HEADS)]
        scores = []
        for h in range(N_HEADS):
            keep = (rc >= 0) if own else (sel_ref[h, pl.ds(j, 1), :] > 0.5)
            scores.append(jnp.where(keep, raw[h] - _alibi_slope(h) * dist, NEG_INF))
        values = [vt_ref[h * HEAD_DIM:(h + 1) * HEAD_DIM, pl.ds(start, blk)] for h in range(N_HEADS)]
        _softmax_steps_t(scores, values, m_ref, l_ref, acc_ref)

    block(qi, True)
    lax.fori_loop(0, qi, lambda j, c: (block(j, False), c)[1], 0)
    _write_attention_out(o_ref, l_ref, acc_ref)


def _moba_prompt_call(qt, kr, vt, kmean):
    b, w, s = qt.shape
    assert s % MOBA_BLOCK == 0
    nb = s // MOBA_BLOCK
    nbp = kmean.shape[1]
    return pl.pallas_call(
        functools.partial(_moba_prompt_kernel, nb=nb),
        grid=(b, nb),
        in_specs=[pl.BlockSpec((None, w, MOBA_BLOCK), lambda i, j: (i, 0, j)),
                  pl.BlockSpec((None, N_HEADS, s, HEAD_DIM), lambda i, j: (i, 0, 0, 0)),
                  pl.BlockSpec((None, w, s), lambda i, j: (i, 0, 0)),
                  pl.BlockSpec((None, nbp, w), lambda i, j: (i, 0, 0))],
        out_specs=pl.BlockSpec((None, MOBA_BLOCK, w), lambda i, j: (i, j, 0)),
        out_shape=jax.ShapeDtypeStruct((b, s, w), _F32),
        scratch_shapes=[pltpu.VMEM((N_HEADS, nbp, MOBA_BLOCK), _F32)] + _softmax_scratch(MOBA_BLOCK),
        compiler_params=_params("parallel", "parallel"),
        name="moba_prompt",
    )(qt, kr, vt, kmean)


def _fox_prompt_kernel(qt_ref, kaug_ref, vt_ref, o_ref, m_ref, l_ref, acc_ref):
    qi = pl.program_id(1)
    tq = qt_ref.shape[1]
    blk = tq
    causal = lax.broadcasted_iota(jnp.int32, (blk, tq), 1) >= lax.broadcasted_iota(jnp.int32, (blk, tq), 0)
    ones_rows = (lax.broadcasted_iota(jnp.int32, (LANES - HEAD_DIM, tq), 0) < FOX_BIAS_TERMS).astype(_BF16)
    qs = []
    for h in range(N_HEADS):
        qt = (qt_ref[h * HEAD_DIM:(h + 1) * HEAD_DIM, :] * (HEAD_DIM ** -0.5)).astype(_BF16)
        qs.append(jnp.concatenate([qt, ones_rows], axis=0))
    _init_softmax_state(m_ref, l_ref, acc_ref)

    def block(j, own):
        start = pl.multiple_of(j * blk, blk)
        scores = [jnp.dot(kaug_ref[h, pl.ds(start, blk), :], qs[h], preferred_element_type=_F32)
                  for h in range(N_HEADS)]
        if own:
            scores = [jnp.where(causal, s, NEG_INF) for s in scores]
        values = [vt_ref[h * HEAD_DIM:(h + 1) * HEAD_DIM, pl.ds(start, blk)] for h in range(N_HEADS)]
        _softmax_steps_t(scores, values, m_ref, l_ref, acc_ref)

    block(qi, True)
    lax.fori_loop(0, qi, lambda j, c: (block(j, False), c)[1], 0)
    _write_attention_out(o_ref, l_ref, acc_ref)


def _fox_prompt_call(qt, kaug, vt):
    b, w, s = qt.shape
    tq = MOBA_BLOCK
    assert s % tq == 0
    return pl.pallas_call(
        _fox_prompt_kernel,
        grid=(b, s // tq),
        in_specs=[pl.BlockSpec((None, w, tq), lambda i, j: (i, 0, j)),
                  pl.BlockSpec((None, N_HEADS, s, LANES), lambda i, j: (i, 0, 0, 0)),
                  pl.BlockSpec((None, w, s), lambda i, j: (i, 0, 0))],
        out_specs=pl.BlockSpec((None, tq, w), lambda i, j: (i, j, 0)),
        out_shape=jax.ShapeDtypeStruct((b, s, w), _F32),
        scratch_shapes=_softmax_scratch(tq),
        compiler_params=_params("parallel", "parallel"),
        name="fox_prompt",
    )(qt, kaug, vt)


def _s5_disc_kernel(lre_ref, lim_ref, ldt_ref, lbre_ref, lbim_ref, fre_ref, fim_ref):
    lre = lre_ref[...]
    lim = lim_ref[...]
    dt = jnp.exp(ldt_ref[...])
    mag = jnp.exp(lre * dt)
    lb_re = mag * jnp.cos(lim * dt)
    lb_im = mag * jnp.sin(lim * dt)
    den = lre * lre + lim * lim
    nr = lb_re - 1.0
    lbre_ref[...] = lb_re
    lbim_ref[...] = lb_im
    fre_ref[...] = (nr * lre + lb_im * lim) / den
    fim_ref[...] = (lb_im * lre - nr * lim) / den


def _s5_bbar_kernel(fre_ref, fim_ref, bre_ref, bim_ref, ore_ref, oim_ref):
    fre, fim, bre, bim = fre_ref[...], fim_ref[...], bre_ref[...], bim_ref[...]
    ore_ref[...] = fre * bre - fim * bim
    oim_ref[...] = fre * bim + fim * bre


def _whole_call(kernel, n_out, shape, name, *args):
    return pl.pallas_call(
        kernel,
        out_shape=[jax.ShapeDtypeStruct(shape, _F32)] * n_out,
        name=name,
    )(*args)


def _s5_kernel(u_ref, h0re_ref, h0im_ref, lbre_ref, lbim_ref, d_ref, *refs, steps, bp):
    n_w = (len(refs) - 5) // 4
    bbd_refs, cre_refs, cim_refs, glu_refs = (refs[i * n_w:(i + 1) * n_w] for i in range(4))
    oc_ref, sre_ref, sim_ref, bu_ref, st_ref = refs[4 * n_w:]
    n = lbre_ref.shape[1]

    @pl.when(pl.program_id(0) == 0)
    def _():
        st_ref[0] = h0re_ref[...]
        st_ref[1] = h0im_ref[...]

    u = u_ref[...]
    bu_ref[...] = _matmul_w(u, bbd_refs)
    a_re = jnp.broadcast_to(lbre_ref[...], (bp, n))
    a_im = jnp.broadcast_to(lbim_ref[...], (bp, n))

    def step(t, carry):
        xr, xi = carry
        rows = pl.ds(pl.multiple_of(t * bp, bp), bp)
        nr = a_re * xr - a_im * xi + bu_ref[rows, 0:n]
        ni = a_re * xi + a_im * xr + bu_ref[rows, n:2 * n]
        bu_ref[rows, 0:n] = nr
        bu_ref[rows, n:2 * n] = ni
        return nr, ni

    xr, xi = lax.fori_loop(0, steps, step, (st_ref[0], st_ref[1]))
    st_ref[0] = xr
    st_ref[1] = xi
    sre_ref[...] = xr
    sim_ref[...] = xi
    y = _matmul_w(bu_ref[:, 0:n], cre_refs) - _matmul_w(bu_ref[:, n:2 * n], cim_refs) + d_ref[...] * u
    yc = jax.nn.gelu(y)
    oc_ref[...] = yc * jax.nn.sigmoid(_matmul_w(yc, glu_refs))


def _s5_call(u_tm, h0re, h0im, lbre, lbim, dskip, bbd, cre, cim, glu, *, steps):
    rows, w = u_tm.shape
    bp, n = h0re.shape
    total_steps = rows // bp
    assert total_steps % steps == 0 and bp % 8 == 0
    r = steps * bp
    st = jax.ShapeDtypeStruct((bp, n), _F32)
    weights = list(bbd) + list(cre) + list(cim) + list(glu)
    return pl.pallas_call(
        functools.partial(_s5_kernel, steps=steps, bp=bp),
        grid=(total_steps // steps,),
        in_specs=[pl.BlockSpec((r, w), lambda c: (c, 0)), _const_spec((bp, n)), _const_spec((bp, n)),
                  _const_spec(lbre.shape), _const_spec(lbim.shape), _const_spec(dskip.shape)]
        + [_const_spec(a.shape) for a in weights],
        out_specs=[pl.BlockSpec((r, w), lambda c: (c, 0)), _const_spec((bp, n)), _const_spec((bp, n))],
        out_shape=[jax.ShapeDtypeStruct((rows, w), _F32), st, st],
        scratch_shapes=[pltpu.VMEM((r, 2 * n), _F32), pltpu.VMEM((2, bp, n), _F32)],
        compiler_params=_params("arbitrary"),
        name="s5_scan",
    )(u_tm, h0re, h0im, lbre, lbim, dskip, *weights)


def _split2(x):
    hi = x.astype(_BF16)
    return hi, (x - hi.astype(_F32)).astype(_BF16)


def _dot_3pass(a, b, dims=None):
    ah, al = _split2(a)
    bh, bl = _split2(b)
    if dims is None:
        d = lambda x, y: jnp.dot(x, y, preferred_element_type=_F32)
    else:
        d = lambda x, y: lax.dot_general(x, y, dims, preferred_element_type=_F32)
    return d(ah, bh) + d(ah, bl) + d(al, bh)


def _dot_hi_lo(x, w_hi, w_lo):
    xh, xl = _split2(x)
    d = lambda a, b: jnp.dot(a, b, preferred_element_type=_F32)
    return d(xh, w_hi) + d(xh, w_lo) + d(xl, w_hi)


def _unit_lower_inverses(nmats, eye, span_limit):
    ts = [eye - n for n in nmats]
    powers = list(nmats)
    span = 2
    while span < span_limit:
        powers = [_dot(p, p) for p in powers]
        ts = [t + _dot(t, p) for t, p in zip(ts, powers)]
        span *= 2
    resids = [eye - _dot_3pass(eye + n, t) for n, t in zip(nmats, ts)]
    return [t + _dot(t, r) for t, r in zip(ts, resids)]


GDN_CHUNKS_PER_STEP = 8


def _gdn_prompt_kernel(x_ref, z_ref, sm_ref, cw_ref, on_ref, seg_ref, tri_ref,
                       o_ref, s_out_ref, ext_ref, state_ref):
    c = GDN_CHUNK
    w = BRANCH_WIDTH
    rows = x_ref.shape[0]
    n = pl.program_id(1)

    @pl.when(n == 0)
    def _():
        ext_ref[0:8, :] = jnp.zeros((8, 3 * w), _F32)
        state_ref[...] = jnp.zeros(state_ref.shape, _F32)

    ext_ref[8:8 + rows, :] = x_ref[...]
    conv = cw_ref[0:1, :] * ext_ref[5:5 + rows, :]
    for j in range(1, GDN_CONV):
        conv = conv + cw_ref[j:j + 1, :] * ext_ref[5 + j:5 + j + rows, :]
    ext_ref[0:8, :] = ext_ref[rows:rows + 8, :]
    act = conv * jax.nn.sigmoid(conv)
    l2 = lambda a: a * lax.rsqrt(_dot_x_exact(a * a, seg_ref[...]) + EPS)
    q_all = l2(act[:, 0:w]) * (HEAD_DIM ** -0.5)
    k_all = l2(act[:, w:2 * w])
    v_all = act[:, 2 * w:3 * w]
    sm_all = sm_ref[...]
    z_all = z_ref[...]
    st = w
    row = lax.broadcasted_iota(jnp.int32, (st, st), 0)
    col = lax.broadcasted_iota(jnp.int32, (st, st), 1)
    same_head = lax.shift_right_logical(row, 6) == lax.shift_right_logical(col, 6)
    eye = (row == col).astype(_F32)
    lower = same_head & (row >= col)
    strict = same_head & (row > col)
    stack = lambda a, rs: jnp.concatenate([a[rs, h * HEAD_DIM:(h + 1) * HEAD_DIM] for h in range(N_HEADS)], axis=0)
    stack_col = lambda a, first: jnp.concatenate([a[:, first + h:first + h + 1] for h in range(N_HEADS)], axis=0)
    block_diag = lambda a: jnp.where(same_head, jnp.concatenate([a] * N_HEADS, axis=1), 0.0)
    chunks = [slice(cc * c, (cc + 1) * c) for cc in range(rows // c)]
    sm = [sm_all[rs, :] for rs in chunks]
    gc_all = [_exact_dot_x(tri_ref[...], x) for x in sm]
    q = [stack(q_all, rs) for rs in chunks]
    k = [stack(k_all, rs) for rs in chunks]
    v = [stack(v_all, rs) for rs in chunks]
    beta = [stack_col(x, N_HEADS) for x in sm]
    gcol = [stack_col(x, 2 * N_HEADS) for x in gc_all]
    g_last = [jnp.concatenate([jnp.broadcast_to(x[c - 1:c, 2 * N_HEADS + h:2 * N_HEADS + h + 1], (c, 1))
                               for h in range(N_HEADS)], axis=0) for x in gc_all]
    decay = []
    for g in gcol:
        grow = jnp.sum(eye * g, axis=0, keepdims=True)
        decay.append(jnp.where(lower, jnp.exp(jnp.where(lower, g - grow, 0.0)), 0.0))
    kb = [a * b for a, b in zip(k, beta)]
    kk = [_dot_nt(a, b) for a, b in zip(kb, k)]
    nmat = [jnp.where(strict, a * d, 0.0) for a, d in zip(kk, decay)]
    t = _unit_lower_inverses(nmat, eye, c)
    u = [_dot(a, b * bt) for a, b, bt in zip(t, v, beta)]
    wmat = [_dot(a, b * jnp.exp(g)) for a, b, g in zip(t, kb, gcol)]
    attn = [_dot_nt(a, b) * d for a, b, d in zip(q, k, decay)]
    kdec_t = [_dot_nt(eye, block_diag(a * jnp.exp(gl - g))) for a, gl, g in zip(k, g_last, gcol)]
    w_bd = [block_diag(a) for a in wmat]
    qg_bd = [block_diag(a * jnp.exp(g)) for a, g in zip(q, gcol)]

    state = state_ref[...]
    for i, rs in enumerate(chunks):
        v_new = u[i] - _dot(w_bd[i], state)
        o = _dot(qg_bd[i], state) + _dot(attn[i], v_new)
        state = state * jnp.exp(g_last[i]) + _dot(kdec_t[i], v_new)
        on = o * lax.rsqrt(jnp.mean(o * o, axis=-1, keepdims=True) + EPS) * on_ref[...]
        for h in range(N_HEADS):
            hs = slice(h * HEAD_DIM, (h + 1) * HEAD_DIM)
            zh = z_all[rs, hs]
            o_ref[rs, hs] = on[h * c:(h + 1) * c, :] * (zh * jax.nn.sigmoid(zh))
    state_ref[...] = state
    s_out_ref[...] = state


def _gdn_prompt_call(qkvd, zd, sm_rows, conv_w, on_gain, seg, tri):
    b, s, w3 = qkvd.shape
    w = w3 // 3
    rows = GDN_CHUNK * GDN_CHUNKS_PER_STEP
    assert s % rows == 0 and HEAD_DIM == GDN_CHUNK and w == N_HEADS * HEAD_DIM
    blk = lambda n: pl.BlockSpec((None, rows, n), lambda i, j: (i, j, 0))
    od, state = pl.pallas_call(
        _gdn_prompt_kernel,
        grid=(b, s // rows),
        in_specs=[blk(w3), blk(w), blk(SMALL_COLS), _const_spec(conv_w.shape), _const_spec(on_gain.shape),
                  _const_spec(seg.shape), _const_spec(tri.shape)],
        out_specs=[blk(w), pl.BlockSpec((None, w, HEAD_DIM), lambda i, j: (i, 0, 0))],
        out_shape=[jax.ShapeDtypeStruct((b, s, w), _F32), jax.ShapeDtypeStruct((b, w, HEAD_DIM), _F32)],
        scratch_shapes=[pltpu.VMEM((rows + 8, w3), _F32), pltpu.VMEM((w, HEAD_DIM), _F32)],
        compiler_params=_params("parallel", "arbitrary"),
        name="gdn_prompt",
    )(qkvd, zd, sm_rows, conv_w, on_gain, seg, tri)
    return od, state.reshape(b, N_HEADS, HEAD_DIM, HEAD_DIM)


def _matmul_w(x, w_refs, index=None, cols=None):
    if index is not None:
        pick = lambda r: r[index]
    elif cols is not None:
        pick = lambda r: r[:, cols]
    else:
        pick = lambda r: r[...]
    if len(w_refs) == 1:
        return _dot(x, pick(w_refs[0]))
    return _dot_hi_lo(x, pick(w_refs[0]), pick(w_refs[1]))


def _merge_kernel(oa_ref, ob_ref, oc_ref, od_ref, x_ref, mod_ref, g1_ref, *refs):
    n_w = (len(refs) - 1) // 3
    wg_refs, wb_refs, wo_refs, y_ref = refs[:n_w], refs[n_w:2 * n_w], refs[2 * n_w:3 * n_w], refs[3 * n_w]
    d = x_ref.shape[1]
    h = _modulated_norm(x_ref, mod_ref, g1_ref, 0, 1)
    mixed = None
    for i, ref in enumerate((oa_ref, ob_ref, oc_ref, od_ref)):
        gate = _matmul_w(h, wg_refs, cols=slice(i * d, (i + 1) * d))
        term = jax.nn.sigmoid(gate) * _matmul_w(ref[...], wb_refs, index=i)
        mixed = term if mixed is None else mixed + term
    y_ref[...] = x_ref[...] + mod_ref[:, 2 * d:3 * d] * _matmul_w(mixed, wo_refs)


def _mlp_kernel(x_ref, mod_ref, g2_ref, *refs):
    n_w = (len(refs) - 1) // 2
    wi_refs, wo_refs, y_ref = refs[:n_w], refs[n_w:2 * n_w], refs[2 * n_w]
    d = x_ref.shape[1]
    h = _modulated_norm(x_ref, mod_ref, g2_ref, 3, 4)
    ff = jnp.maximum(_matmul_w(h, wi_refs), 0.0)
    y_ref[...] = x_ref[...] + mod_ref[:, 5 * d:6 * d] * _matmul_w(ff * ff, wo_refs)


def _merge_call(oa, ob, oc, od, x2d, mod, g1, wg, wb, wo, *, seq, tm):
    t, d = x2d.shape
    row = lambda n: pl.BlockSpec((tm, n), lambda i: (i, 0))
    weights = list(wg) + list(wb) + list(wo)
    return pl.pallas_call(
        _merge_kernel,
        grid=(t // tm,),
        in_specs=[row(BRANCH_WIDTH)] * 4 + [row(d), _mod_spec(mod, seq, tm), _const_spec(g1.shape)]
        + [_const_spec(a.shape) for a in weights],
        out_specs=row(d),
        out_shape=jax.ShapeDtypeStruct((t, d), _F32),
        compiler_params=_params("parallel"),
        name="merge",
    )(oa, ob, oc, od, x2d, mod, g1, *weights)


def _mlp_call(x2d, mod, g2, wi, wo, *, seq, tm):
    t, d = x2d.shape
    row = lambda n: pl.BlockSpec((tm, n), lambda i: (i, 0))
    weights = list(wi) + list(wo)
    return pl.pallas_call(
        _mlp_kernel,
        grid=(t // tm,),
        in_specs=[row(d), _mod_spec(mod, seq, tm), _const_spec(g2.shape)] + [_const_spec(a.shape) for a in weights],
        out_specs=row(d),
        out_shape=jax.ShapeDtypeStruct((t, d), _F32),
        compiler_params=_params("parallel"),
        name="mlp",
    )(x2d, mod, g2, *weights)


PAGES_PER_STEP = 8
SELECT_PAGES_PER_STEP = 16


def _head_rows(row_vec, rows=8):
    shape = (rows, row_vec.shape[1])
    head_of_lane = lax.shift_right_logical(lax.broadcasted_iota(jnp.int32, shape, 1), 6)
    return jnp.where(head_of_lane == lax.broadcasted_iota(jnp.int32, shape, 0), row_vec, 0.0)


def _page_spec(layer, rows, slot_fn):
    return pl.BlockSpec((None, None, rows, LANES), lambda *a: (layer,) + slot_fn(*a))


def _moba_select_kernel(pt_ref, q_ref, *refs, n_blocks):
    pages = refs[:SELECT_PAGES_PER_STEP]
    ones_ref, o_ref, km_ref = refs[SELECT_PAGES_PER_STEP:]
    g = pl.program_id(1)
    blocks_per_step = SELECT_PAGES_PER_STEP // 2

    @pl.when(g == 0)
    def _():
        km_ref[...] = jnp.zeros(km_ref.shape, _F32)

    lane = lax.broadcasted_iota(jnp.int32, km_ref.shape, 1)
    km = km_ref[...]
    for i in range(blocks_per_step):
        both = pages[2 * i][...] + pages[2 * i + 1][...]
        sums = _dot_x_exact(both, ones_ref[...])
        km = jnp.where(lane == g * blocks_per_step + i, sums * (1.0 / MOBA_BLOCK), km)
    km_ref[...] = km

    @pl.when(g == pl.num_programs(1) - 1)
    def _():
        gate = jnp.dot(_head_rows(q_ref[...]), km, precision=_HIGHEST, preferred_element_type=_F32)
        lane8 = lax.broadcasted_iota(jnp.int32, gate.shape, 1)
        valid = lane8 < n_blocks
        gv = jnp.where(valid, gate, NEG_INF)
        cnt = jnp.zeros(gate.shape, _F32)
        for jp in range(n_blocks):
            col = gv[:, jp:jp + 1]
            cnt = cnt + ((col > gv) | ((col == gv) & (jp < lane8))).astype(_F32)
        sel = (valid & (cnt < MOBA_TOPK)).astype(_F32)
        pos = jnp.zeros(gate.shape, _F32)
        for jp in range(n_blocks):
            pos = pos + jnp.where(lane8 > jp, sel[:, jp:jp + 1], 0.0)
        out = jnp.zeros(gate.shape, jnp.int32)
        for slot in range(MOBA_TOPK):
            hit = (sel > 0.5) & (pos == float(slot))
            idx = jnp.sum(jnp.where(hit, lane8.astype(_F32), 0.0), axis=1, keepdims=True)
            out = jnp.where(lane8 == slot, idx.astype(jnp.int32), out)
        o_ref[...] = out


def _moba_select_call(layer, q3, cache_kt, page_table, ones):
    bs = q3.shape[0]
    n_pages = page_table.shape[1]
    per_step = SELECT_PAGES_PER_STEP
    assert n_pages % per_step == 0
    n_blocks = n_pages // 2
    assert MOBA_TOPK <= n_blocks <= LANES
    w = q3.shape[-1]
    page = lambda i: _page_spec(layer, w, lambda b, g, pt: (pt[b, g * per_step + i], 0, 0))
    return pl.pallas_call(
        functools.partial(_moba_select_kernel, n_blocks=n_blocks),
        grid_spec=pltpu.PrefetchScalarGridSpec(
            num_scalar_prefetch=1,
            grid=(bs, n_pages // per_step),
            in_specs=[pl.BlockSpec((None, 1, w), lambda b, g, pt: (b, 0, 0))]
            + [page(i) for i in range(per_step)] + [pl.BlockSpec(ones.shape, lambda b, g, pt: (0, 0))],
            out_specs=pl.BlockSpec((None, 8, LANES), lambda b, g, pt: (b, 0, 0)),
            scratch_shapes=[pltpu.VMEM((w, LANES), _F32)]),
        out_shape=jax.ShapeDtypeStruct((bs, 8, LANES), jnp.int32),
        compiler_params=_params("parallel", "arbitrary"),
        name="moba_select",
    )(page_table, q3, *([cache_kt] * per_step), ones)


def _moba_decode_kernel(pt_ref, sel_ref, q_ref, kn_ref, vn_ref, slope_ref, *refs, past_len):
    n_pg = 2 * MOBA_TOPK
    k_pages, v_pages, o_ref = refs[:n_pg], refs[n_pg:2 * n_pg], refs[2 * n_pg]
    b, h = pl.program_id(0), pl.program_id(1)
    qs = q_ref[...] * (HEAD_DIM ** -0.5)
    qs8 = jnp.broadcast_to(qs, (8, HEAD_DIM))
    slope = slope_ref[...]
    lane = lax.broadcasted_iota(jnp.int32, (1, LANES), 1)
    s_own = jnp.sum(qs * kn_ref[...], axis=1, keepdims=True)
    scores = []
    for i in range(n_pg):
        blk = sel_ref[b, h * MOBA_TOPK + i // 2]
        pos = blk * MOBA_BLOCK + (i % 2) * LANES + lane
        s = _dot_3pass(qs8, k_pages[i][...])[0:1, :]
        scores.append(s - slope * (past_len - pos).astype(_F32))
    m = s_own
    for s in scores:
        m = jnp.maximum(m, jnp.max(s, axis=1, keepdims=True))
    p_own = jnp.exp(s_own - m)
    l = p_own
    acc = p_own * vn_ref[...]
    for s, v_ref in zip(scores, v_pages):
        p = jnp.exp(s - m)
        l = l + jnp.sum(p, axis=1, keepdims=True)
        acc = acc + _dot_3pass(jnp.broadcast_to(p, (8, LANES)), v_ref[...], _NT)[0:1, :]
    o_ref[...] = acc / l


def _moba_decode_call(layer, q4, kn4, vn4, sel, cache_kt, cache_vt, page_table, slopes, past_len):
    bs, nh = q4.shape[:2]
    vec = pl.BlockSpec((None, None, 1, HEAD_DIM), lambda b, h, pt, sl: (b, h, 0, 0))

    def page(i):
        return _page_spec(layer, HEAD_DIM,
                          lambda b, h, pt, sl: (pt[b, 2 * sl[b, h * MOBA_TOPK + i // 2] + i % 2], h, 0))

    pages = [page(i) for i in range(2 * MOBA_TOPK)]
    return pl.pallas_call(
        functools.partial(_moba_decode_kernel, past_len=past_len),
        grid_spec=pltpu.PrefetchScalarGridSpec(
            num_scalar_prefetch=2,
            grid=(bs, nh),
            in_specs=[vec, vec, vec, pl.BlockSpec((None, 1, LANES), lambda b, h, pt, sl: (h, 0, 0))] + pages + pages,
            out_specs=vec),
        out_shape=jax.ShapeDtypeStruct(q4.shape, _F32),
        compiler_params=_params("parallel", "parallel"),
        name="moba_decode",
    )(page_table, sel, q4, kn4, vn4, slopes, *([cache_kt] * len(pages)), *([cache_vt] * len(pages)))


def _fox_decode_kernel(pt_ref, q_ref, kn_ref, vn_ref, smn_ref, *refs):
    n = PAGES_PER_STEP
    k_pages, v_pages, f_pages = refs[:n], refs[n:2 * n], refs[2 * n:3 * n]
    suf_ref, o_ref, qb_ref, m_ref, l_ref, acc_ref, carry_ref = refs[3 * n:]
    gi = pl.program_id(1)
    heads = [slice(h * HEAD_DIM, (h + 1) * HEAD_DIM) for h in range(N_HEADS)]

    @pl.when(gi == 0)
    def _():
        qcol = q_ref[...] * (HEAD_DIM ** -0.5)
        qb_ref[...] = jnp.broadcast_to(qcol, qb_ref.shape)
        prod = qcol * kn_ref[...]
        first = lax.broadcasted_iota(jnp.int32, (1, LANES), 1) == 0
        m_ref[...] = jnp.full(m_ref.shape, NEG_INF, _F32)
        l_ref[...] = jnp.zeros(l_ref.shape, _F32)
        for h, hs in enumerate(heads):
            m_ref[h:h + 1, :] = jnp.where(first, jnp.sum(prod[hs, :], axis=0, keepdims=True), NEG_INF)
            l_ref[h:h + 1, :] = first.astype(_F32)
            acc_ref[hs, :] = jnp.where(first, vn_ref[hs, :], 0.0)
        row8 = lax.broadcasted_iota(jnp.int32, (8, LANES), 0)
        lane8 = lax.broadcasted_iota(jnp.int32, (8, LANES), 1)
        carry_ref[...] = jnp.sum(jnp.where((lane8 == row8) & (row8 < N_HEADS), smn_ref[...], 0.0), axis=1, keepdims=True)

    suffix = carry_ref[...]
    lfs = [f_pages[i][...] for i in range(n)]
    in_page = [_dot_x_exact(lf, suf_ref[...]) for lf in lfs]
    bias = [None] * n
    for i in reversed(range(n)):
        bias[i] = in_page[i] + suffix
        suffix = suffix + (in_page[i][:, 0:1] + lfs[i][:, 0:1])
    carry_ref[...] = suffix
    for h, hs in enumerate(heads):
        qb = qb_ref[hs, :]
        m, l, acc = m_ref[h:h + 1, :], l_ref[h:h + 1, :], acc_ref[hs, :]
        for i in range(n):
            s = jnp.sum(qb * k_pages[i][hs, :], axis=0, keepdims=True) + bias[i][h:h + 1, :]
            m_new = jnp.maximum(m, s)
            alpha = jnp.exp(m - m_new)
            p = jnp.exp(s - m_new)
            l = alpha * l + p
            acc = alpha * acc + p * v_pages[i][hs, :]
            m = m_new
        m_ref[h:h + 1, :], l_ref[h:h + 1, :], acc_ref[hs, :] = m, l, acc

    @pl.when(gi == pl.num_programs(1) - 1)
    def _():
        for h, hs in enumerate(heads):
            m = m_ref[h:h + 1, :]
            wgt = jnp.exp(m - jnp.max(m, axis=1, keepdims=True))
            total = jnp.sum(l_ref[h:h + 1, :] * wgt, axis=1, keepdims=True)
            o_ref[hs, :] = jnp.sum(acc_ref[hs, :] * wgt, axis=1, keepdims=True) / total


def _fox_decode_call(layer, qc, knc, vnc, smn3, cache_kt, cache_vt, cache_lf, page_table, suf):
    bs, w, _ = qc.shape
    n_pages = page_table.shape[1]
    n_groups = n_pages // PAGES_PER_STEP
    assert n_pages % PAGES_PER_STEP == 0

    def page(i, rows):
        return _page_spec(layer, rows, lambda b, g, pt: (pt[b, (n_groups - 1 - g) * PAGES_PER_STEP + i], 0, 0))

    col = pl.BlockSpec((None, w, 1), lambda b, g, pt: (b, 0, 0))
    rng = range(PAGES_PER_STEP)
    return pl.pallas_call(
        _fox_decode_kernel,
        grid_spec=pltpu.PrefetchScalarGridSpec(
            num_scalar_prefetch=1,
            grid=(bs, n_groups),
            in_specs=[col, col, col, pl.BlockSpec((None, 1, SMALL_COLS), lambda b, g, pt: (b, 0, 0))]
            + [page(i, w) for i in rng] + [page(i, w) for i in rng] + [page(i, 8) for i in rng]
            + [pl.BlockSpec(suf.shape, lambda b, g, pt: (0, 0))],
            out_specs=col,
            scratch_shapes=[pltpu.VMEM((w, LANES), _F32), pltpu.VMEM((8, LANES), _F32), pltpu.VMEM((8, LANES), _F32),
                            pltpu.VMEM((w, LANES), _F32), pltpu.VMEM((8, 1), _F32)]),
        out_shape=jax.ShapeDtypeStruct((bs, w, 1), _F32),
        compiler_params=_params("parallel", "arbitrary"),
        name="fox_decode",
    )(page_table, qc, knc, vnc, smn3, *([cache_kt] * PAGES_PER_STEP), *([cache_vt] * PAGES_PER_STEP),
      *([cache_lf] * PAGES_PER_STEP), suf)


def _gdn_decode_kernel(x_ref, buf_ref, z_ref, sm_ref, cw_ref, on_ref, s0_ref, o_ref, s_out_ref):
    w = BRANCH_WIDTH
    conv = cw_ref[GDN_CONV - 1:GDN_CONV, :] * x_ref[...]
    for j in range(GDN_CONV - 1):
        conv = conv + cw_ref[j:j + 1, :] * buf_ref[j:j + 1, :]
    act = conv * jax.nn.sigmoid(conv)
    sm = sm_ref[...]
    z = z_ref[...]
    eye = lax.broadcasted_iota(jnp.int32, (HEAD_DIM, HEAD_DIM), 0) == lax.broadcasted_iota(jnp.int32, (HEAD_DIM, HEAD_DIM), 1)
    to_col = lambda r: jnp.sum(jnp.where(eye, r, 0.0), axis=1, keepdims=True)
    l2 = lambda a: a * lax.rsqrt(jnp.sum(a * a, axis=1, keepdims=True) + EPS)
    for h in range(N_HEADS):
        hs = slice(h * HEAD_DIM, (h + 1) * HEAD_DIM)
        qh = l2(act[:, hs]) * (HEAD_DIM ** -0.5)
        kh = l2(act[:, w + h * HEAD_DIM:w + (h + 1) * HEAD_DIM])
        vh = act[:, 2 * w + h * HEAD_DIM:2 * w + (h + 1) * HEAD_DIM]
        beta = sm[:, N_HEADS + h:N_HEADS + h + 1]
        eg = jnp.exp(sm[:, 2 * N_HEADS + h:2 * N_HEADS + h + 1])
        state = s0_ref[h]
        kcol, qcol = to_col(kh), to_col(qh)
        v_new = beta * (vh - eg * jnp.sum(kcol * state, axis=0, keepdims=True))
        o = eg * jnp.sum(qcol * state, axis=0, keepdims=True) + jnp.sum(qh * kh, axis=1, keepdims=True) * v_new
        s_out_ref[h] = state * eg + kcol * v_new
        on = o * lax.rsqrt(jnp.mean(o * o, axis=-1, keepdims=True) + EPS) * on_ref[...]
        zh = z[:, hs]
        o_ref[:, hs] = on * (zh * jax.nn.sigmoid(zh))


def _gdn_decode_call(x3, buf, z3, sm3, conv_w, on_gain, s0):
    bs = x3.shape[0]
    w = BRANCH_WIDTH
    vec = lambda n: pl.BlockSpec((None, 1, n), lambda b: (b, 0, 0))
    st = pl.BlockSpec((None, N_HEADS, HEAD_DIM, HEAD_DIM), lambda b: (b, 0, 0, 0))
    return pl.pallas_call(
        _gdn_decode_kernel,
        grid=(bs,),
        in_specs=[vec(3 * w), pl.BlockSpec((None, GDN_CONV - 1, 3 * w), lambda b: (b, 0, 0)), vec(w), vec(SMALL_COLS),
                  _const_spec(conv_w.shape), _const_spec(on_gain.shape), st],
        out_specs=[vec(w), st],
        out_shape=[jax.ShapeDtypeStruct((bs, 1, w), _F32), jax.ShapeDtypeStruct(s0.shape, _F32)],
        compiler_params=_params("parallel"),
        name="gdn_decode",
    )(x3, buf, z3, sm3, conv_w, on_gain, s0)


def _constants():
    w = BRANCH_WIDTH
    i = jnp.arange(w)
    c = jnp.arange(GDN_CHUNK)
    return dict(
        seg=(i[:, None] // HEAD_DIM == i[None, :] // HEAD_DIM).astype(_BF16),
        tri_blk=(i[:, None] >= i[None, :]).astype(_BF16),
        tri_l=(c[:, None] >= c[None, :]).astype(_BF16),
        ones=jnp.ones((LANES, LANES), _BF16),
        suf=(jnp.arange(LANES)[:, None] > jnp.arange(LANES)[None, :]).astype(_BF16),
        slopes=jnp.broadcast_to(jnp.asarray([_alibi_slope(h) for h in range(N_HEADS)], _F32)[:, None, None],
                                (N_HEADS, 1, LANES)),
    )


def _block_diag(t):
    l, g, a, b = t.shape
    eye = jnp.eye(g, dtype=bool)
    out = jnp.where(eye[None, :, None, :, None], t[:, :, :, None, :], jnp.zeros((), t.dtype))
    return out.reshape(l, g * a, g * b)


def _prep_weights(p):
    depth, d, _ = p["w_in"].shape
    w = BRANCH_WIDTH
    nh = N_HEADS
    w_in = p["w_in"]
    off_fb = 6 * w
    off_s5 = off_fb + nh
    off_beta = off_s5 + w + 4 * w
    off_gate = off_beta + 2 * nh
    wm = jnp.concatenate([w_in[:, :, 0:off_fb], w_in[:, :, off_s5:off_beta]], axis=-1)
    wg = w_in[:, :, off_gate:]
    ws = jnp.concatenate([w_in[:, :, off_fb:off_s5], w_in[:, :, off_beta:off_gate],
                          jnp.zeros((depth, d, SMALL_COLS - 3 * nh), _F32)], axis=-1)
    tile_h = lambda g: jnp.tile(g, (1, nh))
    zeros_w = jnp.zeros((depth, w), _F32)
    gains = jnp.stack([tile_h(p["moba_qn_g"]), tile_h(p["moba_kn_g"]), tile_h(p["fox_qn_g"]), tile_h(p["fox_kn_g"]),
                       zeros_w, zeros_w, zeros_w, zeros_w], axis=1)
    z4 = jnp.zeros((depth, nh), _F32)
    pad = jnp.zeros((depth, SMALL_COLS - 3 * nh), _F32)
    sp_bias = jnp.concatenate([p["fox_f_bias"], z4, p["gdn_dt_bias"], pad], axis=-1)
    sp_alog = jnp.concatenate([z4, z4, p["gdn_a_log"], pad], axis=-1)
    sp = jnp.concatenate([sp_bias[:, None], sp_alog[:, None], jnp.zeros((depth, 6, SMALL_COLS), _F32)], axis=1)
    wt = wm[:, :, 0:6 * w].transpose(0, 2, 1)
    wst = ws[:, :, 0:16].transpose(0, 2, 1)
    gains_t = gains.transpose(0, 2, 1)
    sp_t = jnp.concatenate([sp[:, 0:2, 0:16].transpose(0, 2, 1), jnp.zeros((depth, 16, 6), _F32)], axis=-1)

    g, s = p["s5_lambda_re"].shape[1:]
    flat = lambda a: a.reshape(depth * g, -1)
    ldt = jnp.broadcast_to(p["s5_log_dt"][:, :, None], (depth, g, s))
    lbre, lbim, fre, fim = _whole_call(_s5_disc_kernel, 4, (depth * g, s), "s5_disc",
                                       flat(p["s5_lambda_re"]), flat(p["s5_lambda_im"]), flat(ldt))
    rep = lambda a: jnp.repeat(a, S5_GROUP, axis=-1)
    bbre, bbim = _whole_call(_s5_bbar_kernel, 2, (depth * g, s * S5_GROUP), "s5_bbar",
                             rep(fre), rep(fim), flat(p["s5_b_re"]), flat(p["s5_b_im"]))
    to_hp = lambda a: a.reshape(depth, g, s, S5_GROUP).transpose(0, 1, 3, 2)
    bbd = jnp.concatenate([_block_diag(to_hp(bbre)), _block_diag(to_hp(bbim))], axis=-1)
    to_ph = lambda a: a.transpose(0, 1, 3, 2)
    cre = _block_diag(to_ph(p["s5_c_re"]))
    cim = _block_diag(to_ph(p["s5_c_im"]))

    conv_w = jnp.concatenate([p["gdn_conv_w"], jnp.zeros((depth, 8 - GDN_CONV, 3 * w), _F32)], axis=1)
    out = dict(
        g1=p["norm1_g"][:, None, :], g2=p["norm2_g"][:, None, :], gains=gains, sp=sp,
        wt=wt.astype(_BF16), wst=wst.astype(_BF16), gains_t=gains_t, sp_t=sp_t,
        lbre=lbre.reshape(depth, 1, g * s), lbim=lbim.reshape(depth, 1, g * s),
        dskip=p["s5_d"][:, None, :], conv_w=conv_w, on_gain=p["gdn_out_g"][:, None, :],
    )
    for name, a in dict(wm=wm, wg=wg, ws=ws, bbd=bbd, cre=cre, cim=cim, glu=p["s5_w_glu"], wb=p["w_branch"], wo=p["w_out"],
                        wi=p["mlp_in"], wo2=p["mlp_out"]).items():
        hi = lax.reduce_precision(a, exponent_bits=8, mantissa_bits=7)
        out[name], out[name + "_lo"] = hi.astype(_BF16), (a - hi).astype(_BF16)
    return out


def _layer_prompt(x2d, mod, lw, cst, batch, seq):
    t = batch * seq
    w = BRANCH_WIDTH
    tm = 256
    r3 = lambda a: a.reshape(batch, seq, a.shape[-1])
    qat, kat, vat, qbt, kbt, vbt, kar, kmean, kbaug, u, qkvd, zd, sm, smt = _inproj_prompt_call(
        x2d, mod, lw, cst["seg"], cst["tri_blk"], batch=batch, seq=seq)
    nb = kmean.shape[1]
    kmean = jnp.pad(kmean.reshape(batch, nb, w), ((0, 0), (0, -nb % 8), (0, 0)))
    oa = _moba_prompt_call(qat, kar, vat, kmean)
    ob = _fox_prompt_call(qbt, kbaug, vbt)
    n = lw["lbre"].shape[-1]
    u_tm = r3(u).transpose(1, 0, 2).reshape(t, w)
    h0 = jnp.zeros((batch, n), _F32)
    oc_tm, sre, sim = _s5_call(u_tm, h0, h0, lw["lbre"], lw["lbim"], lw["dskip"], (lw["bbd"],), (lw["cre"],),
                               (lw["cim"],), (lw["glu"],), steps=64)
    oc = oc_tm.reshape(seq, batch, w).transpose(1, 0, 2).reshape(t, w)
    od, gstate = _gdn_prompt_call(r3(qkvd), r3(zd), r3(sm), lw["conv_w"], lw["on_gain"], cst["seg"], cst["tri_l"])
    x1 = _merge_call(oa.reshape(t, w), ob.reshape(t, w), oc, od.reshape(t, w), x2d, mod, lw["g1"], (lw["wg"],),
                     (lw["wb"],), (lw["wo"],), seq=seq, tm=tm)
    x2 = _mlp_call(x1, mod, lw["g2"], (lw["wi"],), (lw["wo2"],), seq=seq, tm=tm)
    groups = n // S5_STATE
    new = dict(moba_k=kat, moba_v=vat, fox_k=kbt, fox_v=vbt, fox_logf=smt[:, 0:N_HEADS, :],
               s5_re=sre.reshape(batch, groups, S5_STATE), s5_im=sim.reshape(batch, groups, S5_STATE),
               gdn=gstate, gdn_conv=r3(qkvd)[:, seq - (GDN_CONV - 1):, :])
    return x2, new


def _layer_sample(x2d, mod, lw, cst, layer, caches, page_table, past):
    bs = x2d.shape[0]
    w = BRANCH_WIDTH
    nh = N_HEADS
    past_len = page_table.shape[1] * LANES
    qa, ka, va, qb, kb, vb, u, qkvd, zd, sm = _inproj_sample_call(x2d, mod, lw, cst["seg"])
    r3 = lambda a: a.reshape(bs, 1, a.shape[-1])
    r4 = lambda a: a.reshape(bs, nh, 1, HEAD_DIM)
    picked = _moba_select_call(layer, r3(qa), caches["moba_k"], page_table, cst["ones"])
    sel = picked[:, 0:nh, 0:MOBA_TOPK].reshape(bs, nh * MOBA_TOPK)
    oa = _moba_decode_call(layer, r4(qa), r4(ka), r4(va), sel, caches["moba_k"], caches["moba_v"], page_table,
                           cst["slopes"], past_len)
    col = lambda a: a.reshape(bs, w, 1)
    ob = _fox_decode_call(layer, col(qb), col(kb), col(vb), r3(sm), caches["fox_k"], caches["fox_v"],
                          caches["fox_logf"], page_table, cst["suf"])
    n = lw["lbre"].shape[-1]
    pair = lambda name: (lw[name], lw[name + "_lo"])
    oc, sre, sim = _s5_call(u, past["s5_re"].reshape(bs, n), past["s5_im"].reshape(bs, n), lw["lbre"], lw["lbim"],
                            lw["dskip"], pair("bbd"), pair("cre"), pair("cim"), pair("glu"), steps=1)
    od, gstate = _gdn_decode_call(r3(qkvd), past["gdn_conv"], r3(zd), r3(sm), lw["conv_w"], lw["on_gain"], past["gdn"])
    x1 = _merge_call(oa.reshape(bs, w), ob.reshape(bs, w), oc, od.reshape(bs, w), x2d, mod, lw["g1"], pair("wg"),
                     pair("wb"), pair("wo"), seq=1, tm=bs)
    x2 = _mlp_call(x1, mod, lw["g2"], pair("wi"), pair("wo2"), seq=1, tm=bs)
    heads = lambda a: a.reshape(bs, 1, nh, HEAD_DIM)
    new = dict(moba_k=heads(ka), moba_v=heads(va), fox_k=heads(kb), fox_v=heads(vb),
               fox_logf=sm[:, 0:nh].reshape(bs, 1, nh),
               s5_re=sre.reshape(past["s5_re"].shape), s5_im=sim.reshape(past["s5_im"].shape), gdn=gstate,
               gdn_conv=jnp.concatenate([past["gdn_conv"][:, 1:], r3(qkvd)], axis=1))
    return x2, new


_STATE_KEYS = ("moba_k", "moba_v", "fox_k", "fox_v", "fox_logf", "s5_re", "s5_im", "gdn", "gdn_conv")


def kernel(x_prompt, x_sample, cache_moba_k, cache_moba_v, cache_fox_k, cache_fox_v, cache_fox_logf, state_s5_re, state_s5_im, state_gdn, state_gdn_conv, page_table, c_prompt, c_sample, norm1_g, norm2_g, ada_w, ada_b, w_in, moba_qn_g, moba_kn_g, fox_qn_g, fox_kn_g, fox_f_bias, s5_lambda_re, s5_lambda_im, s5_b_re, s5_b_im, s5_c_re, s5_c_im, s5_d, s5_log_dt, s5_w_glu, gdn_conv_w, gdn_a_log, gdn_dt_bias, gdn_out_g, w_branch, w_out, mlp_in, mlp_out):
    batch, seq, d = x_prompt.shape
    bs, dec_seq, _ = x_sample.shape
    depth = w_in.shape[0]
    assert dec_seq == 1 and page_table.shape == (bs, page_table.shape[1]) and cache_moba_k.shape[2] == LANES
    assert (batch + bs) % 8 == 0 and batch % 8 == 0 and bs % 8 == 0
    weights = dict(norm1_g=norm1_g, norm2_g=norm2_g, w_in=w_in, moba_qn_g=moba_qn_g, moba_kn_g=moba_kn_g,
                   fox_qn_g=fox_qn_g, fox_kn_g=fox_kn_g, fox_f_bias=fox_f_bias, s5_lambda_re=s5_lambda_re,
                   s5_lambda_im=s5_lambda_im, s5_b_re=s5_b_re, s5_b_im=s5_b_im, s5_c_re=s5_c_re, s5_c_im=s5_c_im,
                   s5_d=s5_d, s5_log_dt=s5_log_dt, s5_w_glu=s5_w_glu, gdn_conv_w=gdn_conv_w, gdn_a_log=gdn_a_log,
                   gdn_dt_bias=gdn_dt_bias, gdn_out_g=gdn_out_g, w_branch=w_branch, w_out=w_out, mlp_in=mlp_in,
                   mlp_out=mlp_out)
    lw_all = _prep_weights(weights)
    cst = _constants()
    mod_all = _ada_call(jnp.concatenate([c_prompt, c_sample], axis=0), ada_w, ada_b)

    pages_t = lambda c: c.transpose(0, 1, 3, 4, 2).reshape(c.shape[0], c.shape[1], BRANCH_WIDTH, LANES)
    logf_t = jnp.pad(cache_fox_logf.transpose(0, 1, 3, 2), ((0, 0), (0, 0), (0, 8 - N_HEADS), (0, 0)))
    caches = dict(moba_k=pages_t(cache_moba_k), moba_v=pages_t(cache_moba_v), fox_k=pages_t(cache_fox_k),
                  fox_v=pages_t(cache_fox_v), fox_logf=logf_t)

    xp = x_prompt.reshape(batch * seq, d)
    xs = x_sample.reshape(bs, d)
    new_p = {k: [] for k in _STATE_KEYS}
    new_s = {k: [] for k in _STATE_KEYS}
    for layer in range(depth):
        lw = {k: v[layer] for k, v in lw_all.items()}
        xp, st_p = _layer_prompt(xp, mod_all[layer, 0:batch][:, None, :], lw, cst, batch, seq)
        past = dict(s5_re=state_s5_re[layer], s5_im=state_s5_im[layer], gdn=state_gdn[layer],
                    gdn_conv=state_gdn_conv[layer])
        xs, st_s = _layer_sample(xs, mod_all[layer, batch:], lw, cst, layer, caches, page_table, past)
        for k in _STATE_KEYS:
            new_p[k].append(st_p[k])
            new_s[k].append(st_s[k])
    pn = {k: jnp.stack(v) for k, v in new_p.items()}
    sn = {k: jnp.stack(v) for k, v in new_s.items()}
    heads_last = lambda a: a.reshape(depth, batch, N_HEADS, HEAD_DIM, seq).transpose(0, 1, 4, 2, 3)
    return (xp.reshape(batch, seq, d), xs.reshape(bs, 1, d),
            heads_last(pn["moba_k"]), heads_last(pn["moba_v"]), heads_last(pn["fox_k"]), heads_last(pn["fox_v"]),
            pn["fox_logf"].transpose(0, 1, 3, 2), pn["s5_re"], pn["s5_im"], pn["gdn"], pn["gdn_conv"],
            sn["moba_k"], sn["moba_v"], sn["fox_k"], sn["fox_v"], sn["fox_logf"],
            sn["s5_re"], sn["s5_im"], sn["gdn"], sn["gdn_conv"])
```

```python
import functools

import jax
import jax.numpy as jnp
from jax import lax
from jax.experimental import pallas as pl
from jax.experimental.pallas import tpu as pltpu

HEAD_DIM = 64
N_BRANCH = 4
N_HEADS = 4
BRANCH_WIDTH = N_HEADS * HEAD_DIM
MOBA_BLOCK = 256
MOBA_TOPK = 3
S5_GROUP = 16
S5_STATE = 64
GDN_CONV = 4
GDN_CHUNK = 64
N_ADA = 6
EPS = 1e-6
NEG_INF = -1e30
LANES = 128
SMALL_COLS = 128
VMEM_LIMIT_BYTES = 56 * 1024 * 1024

_F32 = jnp.float32
_BF16 = jnp.bfloat16
_HIGHEST = lax.Precision.HIGHEST
_NT = (((1,), (1,)), ((), ()))


def _params(*semantics):
    return pltpu.CompilerParams(dimension_semantics=semantics, vmem_limit_bytes=VMEM_LIMIT_BYTES)


def _dot(a, b):
    return jnp.dot(a.astype(_BF16), b.astype(_BF16), preferred_element_type=_F32)


def _dot_nt(a, b):
    return lax.dot_general(a.astype(_BF16), b.astype(_BF16), _NT, preferred_element_type=_F32)


def _split3(x):
    hi = x.astype(_BF16)
    r1 = x - hi.astype(_F32)
    mid = r1.astype(_BF16)
    lo = (r1 - mid.astype(_F32)).astype(_BF16)
    return hi, mid, lo


def _dot_x_exact(x, m):
    hi, mid, lo = _split3(x)
    d = lambda a: jnp.dot(a, m, preferred_element_type=_F32)
    return d(hi) + d(mid) + d(lo)


def _exact_dot_x(m, x):
    hi, mid, lo = _split3(x)
    d = lambda a: jnp.dot(m, a, preferred_element_type=_F32)
    return d(hi) + d(mid) + d(lo)


def _const_spec(shape):
    nd = len(shape)
    return pl.BlockSpec(shape, lambda *_: (0,) * nd)


def _ada_kernel(c_ref, w_ref, b_ref, o_ref):
    c = c_ref[...]
    o_ref[...] = _dot_3pass(c * jax.nn.sigmoid(c), w_ref[...]) + b_ref[...]


def _ada_call(c_all, ada_w, ada_b):
    depth, d, n = ada_w.shape
    rows = c_all.shape[0]
    tn = n // 4
    return pl.pallas_call(
        _ada_kernel,
        grid=(depth, n // tn),
        in_specs=[pl.BlockSpec((rows, d), lambda l, j: (0, 0)),
                  pl.BlockSpec((None, d, tn), lambda l, j: (l, 0, j)),
                  pl.BlockSpec((None, 1, tn), lambda l, j: (l, 0, j))],
        out_specs=pl.BlockSpec((None, rows, tn), lambda l, j: (l, 0, j)),
        out_shape=jax.ShapeDtypeStruct((depth, rows, n), _F32),
        compiler_params=_params("parallel", "parallel"),
        name="ada_mod",
    )(c_all, ada_w, ada_b.reshape(depth, 1, n))


def _scalar_gates(v_raw, bias, alog, index):
    v = v_raw + bias
    logf = jax.nn.log_sigmoid(v)
    beta = jax.nn.sigmoid(v_raw)
    g = -jnp.exp(alog) * jax.nn.softplus(v)
    nh = N_HEADS
    return jnp.where(index < nh, logf, jnp.where(index < 2 * nh, beta, jnp.where(index < 3 * nh, g, 0.0)))


def _mod_spec(mod, seq, tm):
    if seq > 1:
        tiles_per_batch = seq // tm
        return pl.BlockSpec((None, 1, mod.shape[-1]), lambda i: (i // tiles_per_batch, 0, 0))
    return pl.BlockSpec((tm, mod.shape[-1]), lambda i: (i, 0))


def _modulated_norm(x_ref, mod_ref, g_ref, shift_chunk, scale_chunk):
    d = x_ref.shape[1]
    x = x_ref[...]
    y = x * lax.rsqrt(jnp.mean(x * x, axis=-1, keepdims=True) + EPS) * g_ref[...]
    return y * (1.0 + mod_ref[:, scale_chunk * d:(scale_chunk + 1) * d]) + mod_ref[:, shift_chunk * d:(shift_chunk + 1) * d]


def _head_norm_rows(p, seg_ref, gain_row):
    ss = _dot_x_exact(p * p, seg_ref[...])
    return p * lax.rsqrt(ss * (1.0 / HEAD_DIM) + EPS) * gain_row


def _inproj_common(slab, small, sp_ref, u_ref, qkvd_ref, zd_ref, sm_ref):
    w = BRANCH_WIDTH
    u_ref[...] = slab(6)
    for i in range(3):
        qkvd_ref[:, i * w:(i + 1) * w] = slab(7 + i)
    zd_ref[...] = slab(10)
    out = _scalar_gates(small, sp_ref[0:1, :], sp_ref[1:2, :], lax.broadcasted_iota(jnp.int32, small.shape, 1))
    sm_ref[...] = out
    return out


def _inproj_sample_kernel(x_ref, mod_ref, g1_ref, wm_ref, wml_ref, ws_ref, wsl_ref, gains_ref, sp_ref, seg_ref,
                          qa_ref, ka_ref, va_ref, qb_ref, kb_ref, vb_ref, u_ref, qkvd_ref, zd_ref, sm_ref):
    w = BRANCH_WIDTH
    h = _modulated_norm(x_ref, mod_ref, g1_ref, 0, 1)
    slab = lambda s: _dot_hi_lo(h, wm_ref[:, s * w:(s + 1) * w], wml_ref[:, s * w:(s + 1) * w])
    _inproj_common(slab, _dot_hi_lo(h, ws_ref[...], wsl_ref[...]), sp_ref, u_ref, qkvd_ref, zd_ref, sm_ref)
    qa_ref[...] = _head_norm_rows(slab(0), seg_ref, gains_ref[0:1, :])
    ka_ref[...] = _head_norm_rows(slab(1), seg_ref, gains_ref[1:2, :])
    va_ref[...] = slab(2)
    qb_ref[...] = _head_norm_rows(slab(3), seg_ref, gains_ref[2:3, :])
    kb_ref[...] = _head_norm_rows(slab(4), seg_ref, gains_ref[3:4, :])
    vb_ref[...] = slab(5)


def _inproj_sample_call(x2d, mod, lw, seg):
    t, d = x2d.shape
    w = BRANCH_WIDTH
    row = lambda n: pl.BlockSpec((t, n), lambda i: (i, 0))
    widths = [w] * 7 + [3 * w, w, SMALL_COLS]
    args = [x2d, mod, lw["g1"], lw["wm"], lw["wm_lo"], lw["ws"], lw["ws_lo"], lw["gains"], lw["sp"], seg]
    return pl.pallas_call(
        _inproj_sample_kernel,
        grid=(1,),
        in_specs=[row(d), _mod_spec(mod, 1, t)] + [_const_spec(a.shape) for a in args[2:]],
        out_specs=[row(n) for n in widths],
        out_shape=[jax.ShapeDtypeStruct((t, n), _F32) for n in widths],
        compiler_params=_params("arbitrary"),
        name="inproj_sample",
    )(*args)


FOX_BIAS_TERMS = 3


def _inproj_prompt_kernel(x_ref, mod_ref, g1_ref, wm_ref, ws_ref, gains_ref, sp_ref, seg_ref, wt_ref, wst_ref, gt_ref,
                          spt_ref, tri_ref,
                          qat_ref, kat_ref, vat_ref, qbt_ref, kbt_ref, vbt_ref, kar_ref, kmean_ref, kbaug_ref,
                          u_ref, qkvd_ref, zd_ref, sm_ref, smt_ref, cum_ref):
    w = BRANCH_WIDTH
    hb = _modulated_norm(x_ref, mod_ref, g1_ref, 0, 1).astype(_BF16)
    slab = lambda s: jnp.dot(hb, wm_ref[:, s * w:(s + 1) * w], preferred_element_type=_F32)
    sm = _inproj_common(slab, jnp.dot(hb, ws_ref[...], preferred_element_type=_F32), sp_ref,
                        u_ref, qkvd_ref, zd_ref, sm_ref)

    def slab_t(i):
        return lax.dot_general(wt_ref[i * w:(i + 1) * w, :], hb, _NT, preferred_element_type=_F32)

    def put_head_norm_t(ref, p, col):
        for hd in range(N_HEADS):
            hs = slice(hd * HEAD_DIM, (hd + 1) * HEAD_DIM)
            ph = p[hs, :]
            ms = jnp.mean(ph * ph, axis=0, keepdims=True)
            ref[hs, :] = ph * lax.rsqrt(ms + EPS) * gt_ref[hs, col:col + 1]

    put_head_norm_t(qat_ref, slab_t(0), 0)
    put_head_norm_t(kat_ref, slab_t(1), 1)
    vat_ref[...] = slab_t(2)
    put_head_norm_t(qbt_ref, slab_t(3), 2)
    put_head_norm_t(kbt_ref, slab_t(4), 3)
    vbt_ref[...] = slab_t(5)
    smt = lax.dot_general(wst_ref[...], hb, _NT, preferred_element_type=_F32)
    smt_ref[...] = _scalar_gates(smt, spt_ref[:, 0:1], spt_ref[:, 1:2], lax.broadcasted_iota(jnp.int32, smt.shape, 0))

    ka = _head_norm_rows(slab(1), seg_ref, gains_ref[1:2, :])
    kmean_ref[...] = jnp.mean(ka, axis=0, keepdims=True)
    for hd in range(N_HEADS):
        kar_ref[hd] = ka[:, hd * HEAD_DIM:(hd + 1) * HEAD_DIM].astype(_BF16)

    @pl.when(pl.program_id(1) == 0)
    def _():
        cum_ref[...] = jnp.zeros(cum_ref.shape, _F32)

    cum = _exact_dot_x(tri_ref[...], sm) + cum_ref[...]
    cum_ref[...] = cum[cum.shape[0] - 1:, :]
    kb = _head_norm_rows(slab(4), seg_ref, gains_ref[3:4, :])
    lane = lax.broadcasted_iota(jnp.int32, (kb.shape[0], LANES), 1)
    for hd in range(N_HEADS):
        pair = kb[:, (hd // 2) * LANES:(hd // 2 + 1) * LANES]
        if hd % 2:
            pair = pltpu.roll(pair, HEAD_DIM, axis=1)
        aug = jnp.where(lane < HEAD_DIM, pair, 0.0)
        for i, term in enumerate(_split3(-cum[:, hd:hd + 1])):
            aug = jnp.where(lane == HEAD_DIM + i, term.astype(_F32), aug)
        kbaug_ref[hd] = aug.astype(_BF16)


def _inproj_prompt_call(x2d, mod, lw, seg, tri, *, batch, seq):
    t, d = x2d.shape
    w = BRANCH_WIDTH
    tm = MOBA_BLOCK
    assert seq % tm == 0
    nb = seq // tm
    row = lambda n: pl.BlockSpec((tm, n), lambda b, j: (b * nb + j, 0))
    rs = lambda n: jax.ShapeDtypeStruct((t, n), _F32)
    t_spec = lambda r: pl.BlockSpec((None, r, tm), lambda b, j: (b, 0, j))
    t_shape = lambda r: jax.ShapeDtypeStruct((batch, r, seq), _F32)
    head_rows = lambda n: pl.BlockSpec((None, N_HEADS, tm, n), lambda b, j: (b, 0, j, 0))
    args = [x2d, mod, lw["g1"], lw["wm"], lw["ws"], lw["gains"], lw["sp"], seg, lw["wt"], lw["wst"], lw["gains_t"],
            lw["sp_t"], tri]
    out_specs = [t_spec(w)] * 6 + [head_rows(HEAD_DIM), pl.BlockSpec((None, None, 1, w), lambda b, j: (b, j, 0, 0)),
                                   head_rows(LANES), row(w), row(3 * w), row(w), row(SMALL_COLS), t_spec(16)]
    out_shape = [t_shape(w)] * 6 + [jax.ShapeDtypeStruct((batch, N_HEADS, seq, HEAD_DIM), _BF16),
                                    jax.ShapeDtypeStruct((batch, nb, 1, w), _F32),
                                    jax.ShapeDtypeStruct((batch, N_HEADS, seq, LANES), _BF16),
                                    rs(w), rs(3 * w), rs(w), rs(SMALL_COLS), t_shape(16)]
    return pl.pallas_call(
        _inproj_prompt_kernel,
        grid=(batch, nb),
        in_specs=[row(d), pl.BlockSpec((None, 1, mod.shape[-1]), lambda b, j: (b, 0, 0))]
        + [_const_spec(a.shape) for a in args[2:]],
        out_specs=out_specs,
        out_shape=out_shape,
        scratch_shapes=[pltpu.VMEM((1, SMALL_COLS), _F32)],
        compiler_params=_params("parallel", "arbitrary"),
        name="inproj_prompt",
    )(*args)


def _alibi_slope(h):
    return float(2.0 ** (-8.0 * (h + 1) / N_HEADS))


def _init_softmax_state(m_ref, l_ref, acc_ref):
    m_ref[...] = jnp.full(m_ref.shape, NEG_INF, _F32)
    l_ref[...] = jnp.zeros(l_ref.shape, _F32)
    acc_ref[...] = jnp.zeros(acc_ref.shape, _F32)


def _softmax_steps_t(scores, values, m_ref, l_ref, acc_ref):
    probs, alphas = [], []
    for h, s in enumerate(scores):
        m_old = m_ref[h]
        m_new = jnp.maximum(m_old, jnp.max(s, axis=0, keepdims=True))
        alpha = jnp.exp(m_old - m_new)
        p = jnp.exp(s - m_new)
        l_ref[h] = alpha * l_ref[h] + jnp.sum(p, axis=0, keepdims=True)
        m_ref[h] = m_new
        probs.append(p.astype(_BF16))
        alphas.append(alpha)
    pv = [jnp.dot(v.astype(_BF16), p, preferred_element_type=_F32) for v, p in zip(values, probs)]
    for h in range(len(scores)):
        acc_ref[h] = alphas[h] * acc_ref[h] + pv[h]


def _softmax_scratch(tq):
    return [pltpu.VMEM((N_HEADS, 1, tq), _F32), pltpu.VMEM((N_HEADS, 1, tq), _F32),
            pltpu.VMEM((N_HEADS, HEAD_DIM, tq), _F32)]


def _write_attention_out(o_ref, l_ref, acc_ref):
    out_t = jnp.concatenate([acc_ref[h] / l_ref[h] for h in range(N_HEADS)], axis=0)
    o_ref[...] = out_t.T


def _moba_prompt_kernel(qt_ref, kr_ref, vt_ref, km_ref, o_ref, sel_ref, m_ref, l_ref, acc_ref, *, nb):
    qi = pl.program_id(1)
    tq = qt_ref.shape[1]
    blk = MOBA_BLOCK
    nbp = km_ref.shape[0]
    blk_row = lax.broadcasted_iota(jnp.int32, (nbp, tq), 0)
    rc = (lax.broadcasted_iota(jnp.int32, (blk, tq), 1) - lax.broadcasted_iota(jnp.int32, (blk, tq), 0)).astype(_F32)
    qs = []
    for h in range(N_HEADS):
        hs = slice(h * HEAD_DIM, (h + 1) * HEAD_DIM)
        qt = qt_ref[hs, :]
        gate = jnp.dot(km_ref[:, hs], qt, precision=_HIGHEST, preferred_element_type=_F32)
        valid = blk_row < qi
        gv = jnp.where(valid, gate, NEG_INF)
        cnt = jnp.zeros((nbp, tq), _F32)
        for jp in range(nb):
            other = gv[jp:jp + 1, :]
            cnt = cnt + ((other > gv) | ((other == gv) & (jp < blk_row))).astype(_F32)
        sel_ref[h] = (valid & (cnt < MOBA_TOPK)).astype(_F32)
        qs.append((qt * (HEAD_DIM ** -0.5)).astype(_BF16))
    _init_softmax_state(m_ref, l_ref, acc_ref)

    def block(j, own):
        start = pl.multiple_of(j * blk, blk)
        dist = rc + ((qi - j) * blk).astype(_F32)
        raw = [jnp.dot(kr_ref[h, pl.ds(start, blk), :], qs[h], preferred_element_type=_F32)
               for h in range(N_HEADS)]
        scores = []
        for h in range(N_HEADS):
            keep = (rc >= 0) if own else (sel_ref[h, pl.ds(j, 1), :] > 0.5)
            scores.append(jnp.where(keep, raw[h] - _alibi_slope(h) * dist, NEG_INF))
        values = [vt_ref[h * HEAD_DIM:(h + 1) * HEAD_DIM, pl.ds(start, blk)] for h in range(N_HEADS)]
        _softmax_steps_t(scores, values, m_ref, l_ref, acc_ref)

    block(qi, True)
    lax.fori_loop(0, qi, lambda j, c: (block(j, False), c)[1], 0)
    _write_attention_out(o_ref, l_ref, acc_ref)


def _moba_prompt_call(qt, kr, vt, kmean):
    b, w, s = qt.shape
    assert s % MOBA_BLOCK == 0
    nb = s // MOBA_BLOCK
    nbp = kmean.shape[1]
    return pl.pallas_call(
        functools.partial(_moba_prompt_kernel, nb=nb),
        grid=(b, nb),
        in_specs=[pl.BlockSpec((None, w, MOBA_BLOCK), lambda i, j: (i, 0, j)),
                  pl.BlockSpec((None, N_HEADS, s, HEAD_DIM), lambda i, j: (i, 0, 0, 0)),
                  pl.BlockSpec((None, w, s), lambda i, j: (i, 0, 0)),
                  pl.BlockSpec((None, nbp, w), lambda i, j: (i, 0, 0))],
        out_specs=pl.BlockSpec((None, MOBA_BLOCK, w), lambda i, j: (i, j, 0)),
        out_shape=jax.ShapeDtypeStruct((b, s, w), _F32),
        scratch_shapes=[pltpu.VMEM((N_HEADS, nbp, MOBA_BLOCK), _F32)] + _softmax_scratch(MOBA_BLOCK),
        compiler_params=_params("parallel", "parallel"),
        name="moba_prompt",
    )(qt, kr, vt, kmean)


def _fox_prompt_kernel(qt_ref, kaug_ref, vt_ref, o_ref, m_ref, l_ref, acc_ref):
    qi = pl.program_id(1)
    tq = qt_ref.shape[1]
    blk = tq
    causal = lax.broadcasted_iota(jnp.int32, (blk, tq), 1) >= lax.broadcasted_iota(jnp.int32, (blk, tq), 0)
    ones_rows = (lax.broadcasted_iota(jnp.int32, (LANES - HEAD_DIM, tq), 0) < FOX_BIAS_TERMS).astype(_BF16)
    qs = []
    for h in range(N_HEADS):
        qt = (qt_ref[h * HEAD_DIM:(h + 1) * HEAD_DIM, :] * (HEAD_DIM ** -0.5)).astype(_BF16)
        qs.append(jnp.concatenate([qt, ones_rows], axis=0))
    _init_softmax_state(m_ref, l_ref, acc_ref)

    def block(j, own):
        start = pl.multiple_of(j * blk, blk)
        scores = [jnp.dot(kaug_ref[h, pl.ds(start, blk), :], qs[h], preferred_element_type=_F32)
                  for h in range(N_HEADS)]
        if own:
            scores = [jnp.where(causal, s, NEG_INF) for s in scores]
        values = [vt_ref[h * HEAD_DIM:(h + 1) * HEAD_DIM, pl.ds(start, blk)] for h in range(N_HEADS)]
        _softmax_steps_t(scores, values, m_ref, l_ref, acc_ref)

    block(qi, True)
    lax.fori_loop(0, qi, lambda j, c: (block(j, False), c)[1], 0)
    _write_attention_out(o_ref, l_ref, acc_ref)


def _fox_prompt_call(qt, kaug, vt):
    b, w, s = qt.shape
    tq = MOBA_BLOCK
    assert s % tq == 0
    return pl.pallas_call(
        _fox_prompt_kernel,
        grid=(b, s // tq),
        in_specs=[pl.BlockSpec((None, w, tq), lambda i, j: (i, 0, j)),
                  pl.BlockSpec((None, N_HEADS, s, LANES), lambda i, j: (i, 0, 0, 0)),
                  pl.BlockSpec((None, w, s), lambda i, j: (i, 0, 0))],
        out_specs=pl.BlockSpec((None, tq, w), lambda i, j: (i, j, 0)),
        out_shape=jax.ShapeDtypeStruct((b, s, w), _F32),
        scratch_shapes=_softmax_scratch(tq),
        compiler_params=_params("parallel", "parallel"),
        name="fox_prompt",
    )(qt, kaug, vt)


def _s5_disc_kernel(lre_ref, lim_ref, ldt_ref, lbre_ref, lbim_ref, fre_ref, fim_ref):
    lre = lre_ref[...]
    lim = lim_ref[...]
    dt = jnp.exp(ldt_ref[...])
    mag = jnp.exp(lre * dt)
    lb_re = mag * jnp.cos(lim * dt)
    lb_im = mag * jnp.sin(lim * dt)
    den = lre * lre + lim * lim
    nr = lb_re - 1.0
    lbre_ref[...] = lb_re
    lbim_ref[...] = lb_im
    fre_ref[...] = (nr * lre + lb_im * lim) / den
    fim_ref[...] = (lb_im * lre - nr * lim) / den


def _s5_bbar_kernel(fre_ref, fim_ref, bre_ref, bim_ref, ore_ref, oim_ref):
    fre, fim, bre, bim = fre_ref[...], fim_ref[...], bre_ref[...], bim_ref[...]
    ore_ref[...] = fre * bre - fim * bim
    oim_ref[...] = fre * bim + fim * bre


def _whole_call(kernel, n_out, shape, name, *args):
    return pl.pallas_call(
        kernel,
        out_shape=[jax.ShapeDtypeStruct(shape, _F32)] * n_out,
        name=name,
    )(*args)


def _s5_kernel(u_ref, h0re_ref, h0im_ref, lbre_ref, lbim_ref, d_ref, *refs, steps, bp):
    n_w = (len(refs) - 5) // 4
    bbd_refs, cre_refs, cim_refs, glu_refs = (refs[i * n_w:(i + 1) * n_w] for i in range(4))
    oc_ref, sre_ref, sim_ref, bu_ref, st_ref = refs[4 * n_w:]
    n = lbre_ref.shape[1]

    @pl.when(pl.program_id(0) == 0)
    def _():
        st_ref[0] = h0re_ref[...]
        st_ref[1] = h0im_ref[...]

    u = u_ref[...]
    bu_ref[...] = _matmul_w(u, bbd_refs)
    a_re = jnp.broadcast_to(lbre_ref[...], (bp, n))
    a_im = jnp.broadcast_to(lbim_ref[...], (bp, n))

    def step(t, carry):
        xr, xi = carry
        rows = pl.ds(pl.multiple_of(t * bp, bp), bp)
        nr = a_re * xr - a_im * xi + bu_ref[rows, 0:n]
        ni = a_re * xi + a_im * xr + bu_ref[rows, n:2 * n]
        bu_ref[rows, 0:n] = nr
        bu_ref[rows, n:2 * n] = ni
        return nr, ni

    xr, xi = lax.fori_loop(0, steps, step, (st_ref[0], st_ref[1]))
    st_ref[0] = xr
    st_ref[1] = xi
    sre_ref[...] = xr
    sim_ref[...] = xi
    y = _matmul_w(bu_ref[:, 0:n], cre_refs) - _matmul_w(bu_ref[:, n:2 * n], cim_refs) + d_ref[...] * u
    yc = jax.nn.gelu(y)
    oc_ref[...] = yc * jax.nn.sigmoid(_matmul_w(yc, glu_refs))


def _s5_call(u_tm, h0re, h0im, lbre, lbim, dskip, bbd, cre, cim, glu, *, steps):
    rows, w = u_tm.shape
    bp, n = h0re.shape
    total_steps = rows // bp
    assert total_steps % steps == 0 and bp % 8 == 0
    r = steps * bp
    st = jax.ShapeDtypeStruct((bp, n), _F32)
    weights = list(bbd) + list(cre) + list(cim) + list(glu)
    return pl.pallas_call(
        functools.partial(_s5_kernel, steps=steps, bp=bp),
        grid=(total_steps // steps,),
        in_specs=[pl.BlockSpec((r, w), lambda c: (c, 0)), _const_spec((bp, n)), _const_spec((bp, n)),
                  _const_spec(lbre.shape), _const_spec(lbim.shape), _const_spec(dskip.shape)]
        + [_const_spec(a.shape) for a in weights],
        out_specs=[pl.BlockSpec((r, w), lambda c: (c, 0)), _const_spec((bp, n)), _const_spec((bp, n))],
        out_shape=[jax.ShapeDtypeStruct((rows, w), _F32), st, st],
        scratch_shapes=[pltpu.VMEM((r, 2 * n), _F32), pltpu.VMEM((2, bp, n), _F32)],
        compiler_params=_params("arbitrary"),
        name="s5_scan",
    )(u_tm, h0re, h0im, lbre, lbim, dskip, *weights)


def _split2(x):
    hi = x.astype(_BF16)
    return hi, (x - hi.astype(_F32)).astype(_BF16)


def _dot_3pass(a, b, dims=None):
    ah, al = _split2(a)
    bh, bl = _split2(b)
    if dims is None:
        d = lambda x, y: jnp.dot(x, y, preferred_element_type=_F32)
    else:
        d = lambda x, y: lax.dot_general(x, y, dims, preferred_element_type=_F32)
    return d(ah, bh) + d(ah, bl) + d(al, bh)


def _dot_hi_lo(x, w_hi, w_lo):
    xh, xl = _split2(x)
    d = lambda a, b: jnp.dot(a, b, preferred_element_type=_F32)
    return d(xh, w_hi) + d(xh, w_lo) + d(xl, w_hi)


def _unit_lower_inverses(nmats, eye, span_limit):
    ts = [eye - n for n in nmats]
    powers = list(nmats)
    span = 2
    while span < span_limit:
        powers = [_dot(p, p) for p in powers]
        ts = [t + _dot(t, p) for t, p in zip(ts, powers)]
        span *= 2
    resids = [eye - _dot_3pass(eye + n, t) for n, t in zip(nmats, ts)]
    return [t + _dot(t, r) for t, r in zip(ts, resids)]


GDN_CHUNKS_PER_STEP = 8


def _gdn_prompt_kernel(x_ref, z_ref, sm_ref, cw_ref, on_ref, seg_ref, tri_ref,
                       o_ref, s_out_ref, ext_ref, state_ref):
    c = GDN_CHUNK
    w = BRANCH_WIDTH
    rows = x_ref.shape[0]
    n = pl.program_id(1)

    @pl.when(n == 0)
    def _():
        ext_ref[0:8, :] = jnp.zeros((8, 3 * w), _F32)
        state_ref[...] = jnp.zeros(state_ref.shape, _F32)

    ext_ref[8:8 + rows, :] = x_ref[...]
    conv = cw_ref[0:1, :] * ext_ref[5:5 + rows, :]
    for j in range(1, GDN_CONV):
        conv = conv + cw_ref[j:j + 1, :] * ext_ref[5 + j:5 + j + rows, :]
    ext_ref[0:8, :] = ext_ref[rows:rows + 8, :]
    act = conv * jax.nn.sigmoid(conv)
    l2 = lambda a: a * lax.rsqrt(_dot_x_exact(a * a, seg_ref[...]) + EPS)
    q_all = l2(act[:, 0:w]) * (HEAD_DIM ** -0.5)
    k_all = l2(act[:, w:2 * w])
    v_all = act[:, 2 * w:3 * w]
    sm_all = sm_ref[...]
    z_all = z_ref[...]
    st = w
    row = lax.broadcasted_iota(jnp.int32, (st, st), 0)
    col = lax.broadcasted_iota(jnp.int32, (st, st), 1)
    same_head = lax.shift_right_logical(row, 6) == lax.shift_right_logical(col, 6)
    eye = (row == col).astype(_F32)
    lower = same_head & (row >= col)
    strict = same_head & (row > col)
    stack = lambda a, rs: jnp.concatenate([a[rs, h * HEAD_DIM:(h + 1) * HEAD_DIM] for h in range(N_HEADS)], axis=0)
    stack_col = lambda a, first: jnp.concatenate([a[:, first + h:first + h + 1] for h in range(N_HEADS)], axis=0)
    block_diag = lambda a: jnp.where(same_head, jnp.concatenate([a] * N_HEADS, axis=1), 0.0)
    chunks = [slice(cc * c, (cc + 1) * c) for cc in range(rows // c)]
    sm = [sm_all[rs, :] for rs in chunks]
    gc_all = [_exact_dot_x(tri_ref[...], x) for x in sm]
    q = [stack(q_all, rs) for rs in chunks]
    k = [stack(k_all, rs) for rs in chunks]
    v = [stack(v_all, rs) for rs in chunks]
    beta = [stack_col(x, N_HEADS) for x in sm]
    gcol = [stack_col(x, 2 * N_HEADS) for x in gc_all]
    g_last = [jnp.concatenate([jnp.broadcast_to(x[c - 1:c, 2 * N_HEADS + h:2 * N_HEADS + h + 1], (c, 1))
                               for h in range(N_HEADS)], axis=0) for x in gc_all]
    decay = []
    for g in gcol:
        grow = jnp.sum(eye * g, axis=0, keepdims=True)
        decay.append(jnp.where(lower, jnp.exp(jnp.where(lower, g - grow, 0.0)), 0.0))
    kb = [a * b for a, b in zip(k, beta)]
    kk = [_dot_nt(a, b) for a, b in zip(kb, k)]
    nmat = [jnp.where(strict, a * d, 0.0) for a, d in zip(kk, decay)]
    t = _unit_lower_inverses(nmat, eye, c)
    u = [_dot(a, b * bt) for a, b, bt in zip(t, v, beta)]
    wmat = [_dot(a, b * jnp.exp(g)) for a, b, g in zip(t, kb, gcol)]
    attn = [_dot_nt(a, b) * d for a, b, d in zip(q, k, decay)]
    kdec_t = [_dot_nt(eye, block_diag(a * jnp.exp(gl - g))) for a, gl, g in zip(k, g_last, gcol)]
    w_bd = [block_diag(a) for a in wmat]
    qg_bd = [block_diag(a * jnp.exp(g)) for a, g in zip(q, gcol)]

    state = state_ref[...]
    for i, rs in enumerate(chunks):
        v_new = u[i] - _dot(w_bd[i], state)
        o = _dot(qg_bd[i], state) + _dot(attn[i], v_new)
        state = state * jnp.exp(g_last[i]) + _dot(kdec_t[i], v_new)
        on = o * lax.rsqrt(jnp.mean(o * o, axis=-1, keepdims=True) + EPS) * on_ref[...]
        for h in range(N_HEADS):
            hs = slice(h * HEAD_DIM, (h + 1) * HEAD_DIM)
            zh = z_all[rs, hs]
            o_ref[rs, hs] = on[h * c:(h + 1) * c, :] * (zh * jax.nn.sigmoid(zh))
    state_ref[...] = state
    s_out_ref[...] = state


def _gdn_prompt_call(qkvd, zd, sm_rows, conv_w, on_gain, seg, tri):
    b, s, w3 = qkvd.shape
    w = w3 // 3
    rows = GDN_CHUNK * GDN_CHUNKS_PER_STEP
    assert s % rows == 0 and HEAD_DIM == GDN_CHUNK and w == N_HEADS * HEAD_DIM
    blk = lambda n: pl.BlockSpec((None, rows, n), lambda i, j: (i, j, 0))
    od, state = pl.pallas_call(
        _gdn_prompt_kernel,
        grid=(b, s // rows),
        in_specs=[blk(w3), blk(w), blk(SMALL_COLS), _const_spec(conv_w.shape), _const_spec(on_gain.shape),
                  _const_spec(seg.shape), _const_spec(tri.shape)],
        out_specs=[blk(w), pl.BlockSpec((None, w, HEAD_DIM), lambda i, j: (i, 0, 0))],
        out_shape=[jax.ShapeDtypeStruct((b, s, w), _F32), jax.ShapeDtypeStruct((b, w, HEAD_DIM), _F32)],
        scratch_shapes=[pltpu.VMEM((rows + 8, w3), _F32), pltpu.VMEM((w, HEAD_DIM), _F32)],
        compiler_params=_params("parallel", "arbitrary"),
        name="gdn_prompt",
    )(qkvd, zd, sm_rows, conv_w, on_gain, seg, tri)
    return od, state.reshape(b, N_HEADS, HEAD_DIM, HEAD_DIM)


def _matmul_w(x, w_refs, index=None, cols=None):
    if index is not None:
        pick = lambda r: r[index]
    elif cols is not None:
        pick = lambda r: r[:, cols]
    else:
        pick = lambda r: r[...]
    if len(w_refs) == 1:
        return _dot(x, pick(w_refs[0]))
    return _dot_hi_lo(x, pick(w_refs[0]), pick(w_refs[1]))


def _merge_kernel(oa_ref, ob_ref, oc_ref, od_ref, x_ref, mod_ref, g1_ref, *refs):
    n_w = (len(refs) - 1) // 3
    wg_refs, wb_refs, wo_refs, y_ref = refs[:n_w], refs[n_w:2 * n_w], refs[2 * n_w:3 * n_w], refs[3 * n_w]
    d = x_ref.shape[1]
    h = _modulated_norm(x_ref, mod_ref, g1_ref, 0, 1)
    mixed = None
    for i, ref in enumerate((oa_ref, ob_ref, oc_ref, od_ref)):
        gate = _matmul_w(h, wg_refs, cols=slice(i * d, (i + 1) * d))
        term = jax.nn.sigmoid(gate) * _matmul_w(ref[...], wb_refs, index=i)
        mixed = term if mixed is None else mixed + term
    y_ref[...] = x_ref[...] + mod_ref[:, 2 * d:3 * d] * _matmul_w(mixed, wo_refs)


def _mlp_kernel(x_ref, mod_ref, g2_ref, *refs):
    n_w = (len(refs) - 1) // 2
    wi_refs, wo_refs, y_ref = refs[:n_w], refs[n_w:2 * n_w], refs[2 * n_w]
    d = x_ref.shape[1]
    h = _modulated_norm(x_ref, mod_ref, g2_ref, 3, 4)
    ff = jnp.maximum(_matmul_w(h, wi_refs), 0.0)
    y_ref[...] = x_ref[...] + mod_ref[:, 5 * d:6 * d] * _matmul_w(ff * ff, wo_refs)


def _merge_call(oa, ob, oc, od, x2d, mod, g1, wg, wb, wo, *, seq, tm):
    t, d = x2d.shape
    row = lambda n: pl.BlockSpec((tm, n), lambda i: (i, 0))
    weights = list(wg) + list(wb) + list(wo)
    return pl.pallas_call(
        _merge_kernel,
        grid=(t // tm,),
        in_specs=[row(BRANCH_WIDTH)] * 4 + [row(d), _mod_spec(mod, seq, tm), _const_spec(g1.shape)]
        + [_const_spec(a.shape) for a in weights],
        out_specs=row(d),
        out_shape=jax.ShapeDtypeStruct((t, d), _F32),
        compiler_params=_params("parallel"),
        name="merge",
    )(oa, ob, oc, od, x2d, mod, g1, *weights)


def _mlp_call(x2d, mod, g2, wi, wo, *, seq, tm):
    t, d = x2d.shape
    row = lambda n: pl.BlockSpec((tm, n), lambda i: (i, 0))
    weights = list(wi) + list(wo)
    return pl.pallas_call(
        _mlp_kernel,
        grid=(t // tm,),
        in_specs=[row(d), _mod_spec(mod, seq, tm), _const_spec(g2.shape)] + [_const_spec(a.shape) for a in weights],
        out_specs=row(d),
        out_shape=jax.ShapeDtypeStruct((t, d), _F32),
        compiler_params=_params("parallel"),
        name="mlp",
    )(x2d, mod, g2, *weights)


PAGES_PER_STEP = 16
SELECT_PAGES_PER_STEP = 16


def _head_rows(row_vec, rows=8):
    shape = (rows, row_vec.shape[1])
    head_of_lane = lax.shift_right_logical(lax.broadcasted_iota(jnp.int32, shape, 1), 6)
    return jnp.where(head_of_lane == lax.broadcasted_iota(jnp.int32, shape, 0), row_vec, 0.0)


def _page_spec(layer, rows, slot_fn):
    return pl.BlockSpec((None, None, rows, LANES), lambda *a: (layer,) + slot_fn(*a))


def _moba_select_kernel(pt_ref, q_ref, *refs, n_blocks):
    pages = refs[:SELECT_PAGES_PER_STEP]
    ones_ref, o_ref, km_ref = refs[SELECT_PAGES_PER_STEP:]
    g = pl.program_id(1)
    blocks_per_step = SELECT_PAGES_PER_STEP // 2

    @pl.when(g == 0)
    def _():
        km_ref[...] = jnp.zeros(km_ref.shape, _F32)

    lane = lax.broadcasted_iota(jnp.int32, km_ref.shape, 1)
    km = km_ref[...]
    for i in range(blocks_per_step):
        both = pages[2 * i][...] + pages[2 * i + 1][...]
        sums = _dot_x_exact(both, ones_ref[...])
        km = jnp.where(lane == g * blocks_per_step + i, sums * (1.0 / MOBA_BLOCK), km)
    km_ref[...] = km

    @pl.when(g == pl.num_programs(1) - 1)
    def _():
        gate = jnp.dot(_head_rows(q_ref[...]), km, precision=_HIGHEST, preferred_element_type=_F32)
        lane8 = lax.broadcasted_iota(jnp.int32, gate.shape, 1)
        valid = lane8 < n_blocks
        gv = jnp.where(valid, gate, NEG_INF)
        cnt = jnp.zeros(gate.shape, _F32)
        for jp in range(n_blocks):
            col = gv[:, jp:jp + 1]
            cnt = cnt + ((col > gv) | ((col == gv) & (jp < lane8))).astype(_F32)
        sel = (valid & (cnt < MOBA_TOPK)).astype(_F32)
        pos = jnp.zeros(gate.shape, _F32)
        for jp in range(n_blocks):
            pos = pos + jnp.where(lane8 > jp, sel[:, jp:jp + 1], 0.0)
        out = jnp.zeros(gate.shape, jnp.int32)
        for slot in range(MOBA_TOPK):
            hit = (sel > 0.5) & (pos == float(slot))
            idx = jnp.sum(jnp.where(hit, lane8.astype(_F32), 0.0), axis=1, keepdims=True)
            out = jnp.where(lane8 == slot, idx.astype(jnp.int32), out)
        o_ref[...] = out


def _moba_select_call(layer, q3, cache_kt, page_table, ones):
    bs = q3.shape[0]
    n_pages = page_table.shape[1]
    per_step = SELECT_PAGES_PER_STEP
    assert n_pages % per_step == 0
    n_blocks = n_pages // 2
    assert MOBA_TOPK <= n_blocks <= LANES
    w = q3.shape[-1]
    page = lambda i: _page_spec(layer, w, lambda b, g, pt: (pt[b, g * per_step + i], 0, 0))
    return pl.pallas_call(
        functools.partial(_moba_select_kernel, n_blocks=n_blocks),
        grid_spec=pltpu.PrefetchScalarGridSpec(
            num_scalar_prefetch=1,
            grid=(bs, n_pages // per_step),
            in_specs=[pl.BlockSpec((None, 1, w), lambda b, g, pt: (b, 0, 0))]
            + [page(i) for i in range(per_step)] + [pl.BlockSpec(ones.shape, lambda b, g, pt: (0, 0))],
            out_specs=pl.BlockSpec((None, 8, LANES), lambda b, g, pt: (b, 0, 0)),
            scratch_shapes=[pltpu.VMEM((w, LANES), _F32)]),
        out_shape=jax.ShapeDtypeStruct((bs, 8, LANES), jnp.int32),
        compiler_params=_params("parallel", "arbitrary"),
        name="moba_select",
    )(page_table, q3, *([cache_kt] * per_step), ones)


def _moba_decode_kernel(pt_ref, sel_ref, q_ref, kn_ref, vn_ref, slope_ref, *refs, past_len):
    n_pg = 2 * MOBA_TOPK
    k_pages, v_pages, o_ref = refs[:n_pg], refs[n_pg:2 * n_pg], refs[2 * n_pg]
    b, h = pl.program_id(0), pl.program_id(1)
    qs = q_ref[...] * (HEAD_DIM ** -0.5)
    qs8 = jnp.broadcast_to(qs, (8, HEAD_DIM))
    slope = slope_ref[...]
    lane = lax.broadcasted_iota(jnp.int32, (1, LANES), 1)
    s_own = jnp.sum(qs * kn_ref[...], axis=1, keepdims=True)
    scores = []
    for i in range(n_pg):
        blk = sel_ref[b, h * MOBA_TOPK + i // 2]
        pos = blk * MOBA_BLOCK + (i % 2) * LANES + lane
        s = _dot_3pass(qs8, k_pages[i][...])[0:1, :]
        scores.append(s - slope * (past_len - pos).astype(_F32))
    m = s_own
    for s in scores:
        m = jnp.maximum(m, jnp.max(s, axis=1, keepdims=True))
    p_own = jnp.exp(s_own - m)
    l = p_own
    acc = p_own * vn_ref[...]
    for s, v_ref in zip(scores, v_pages):
        p = jnp.exp(s - m)
        l = l + jnp.sum(p, axis=1, keepdims=True)
        acc = acc + _dot_3pass(jnp.broadcast_to(p, (8, LANES)), v_ref[...], _NT)[0:1, :]
    o_ref[...] = acc / l


def _moba_decode_call(layer, q4, kn4, vn4, sel, cache_kt, cache_vt, page_table, slopes, past_len):
    bs, nh = q4.shape[:2]
    vec = pl.BlockSpec((None, None, 1, HEAD_DIM), lambda b, h, pt, sl: (b, h, 0, 0))

    def page(i):
        return _page_spec(layer, HEAD_DIM,
                          lambda b, h, pt, sl: (pt[b, 2 * sl[b, h * MOBA_TOPK + i // 2] + i % 2], h, 0))

    pages = [page(i) for i in range(2 * MOBA_TOPK)]
    return pl.pallas_call(
        functools.partial(_moba_decode_kernel, past_len=past_len),
        grid_spec=pltpu.PrefetchScalarGridSpec(
            num_scalar_prefetch=2,
            grid=(bs, nh),
            in_specs=[vec, vec, vec, pl.BlockSpec((None, 1, LANES), lambda b, h, pt, sl: (h, 0, 0))] + pages + pages,
            out_specs=vec),
        out_shape=jax.ShapeDtypeStruct(q4.shape, _F32),
        compiler_params=_params("parallel", "parallel"),
        name="moba_decode",
    )(page_table, sel, q4, kn4, vn4, slopes, *([cache_kt] * len(pages)), *([cache_vt] * len(pages)))


def _fox_decode_kernel(pt_ref, q_ref, kn_ref, vn_ref, smn_ref, *refs):
    n = PAGES_PER_STEP
    k_pages, v_pages, f_pages = refs[:n], refs[n:2 * n], refs[2 * n:3 * n]
    suf_ref, o_ref, qb_ref, m_ref, l_ref, acc_ref, carry_ref = refs[3 * n:]
    gi = pl.program_id(1)
    heads = [slice(h * HEAD_DIM, (h + 1) * HEAD_DIM) for h in range(N_HEADS)]

    @pl.when(gi == 0)
    def _():
        qcol = q_ref[...] * (HEAD_DIM ** -0.5)
        qb_ref[...] = jnp.broadcast_to(qcol, qb_ref.shape)
        prod = qcol * kn_ref[...]
        first = lax.broadcasted_iota(jnp.int32, (1, LANES), 1) == 0
        m_ref[...] = jnp.full(m_ref.shape, NEG_INF, _F32)
        l_ref[...] = jnp.zeros(l_ref.shape, _F32)
        for h, hs in enumerate(heads):
            m_ref[h:h + 1, :] = jnp.where(first, jnp.sum(prod[hs, :], axis=0, keepdims=True), NEG_INF)
            l_ref[h:h + 1, :] = first.astype(_F32)
            acc_ref[hs, :] = jnp.where(first, vn_ref[hs, :], 0.0)
        row8 = lax.broadcasted_iota(jnp.int32, (8, LANES), 0)
        lane8 = lax.broadcasted_iota(jnp.int32, (8, LANES), 1)
        carry_ref[...] = jnp.sum(jnp.where((lane8 == row8) & (row8 < N_HEADS), smn_ref[...], 0.0), axis=1, keepdims=True)

    suffix = carry_ref[...]
    lfs = [f_pages[i][...] for i in range(n)]
    in_page = [_dot_x_exact(lf, suf_ref[...]) for lf in lfs]
    bias = [None] * n
    for i in reversed(range(n)):
        bias[i] = in_page[i] + suffix
        suffix = suffix + (in_page[i][:, 0:1] + lfs[i][:, 0:1])
    carry_ref[...] = suffix
    for h, hs in enumerate(heads):
        qb = qb_ref[hs, :]
        scores = [jnp.sum(qb * k_pages[i][hs, :], axis=0, keepdims=True) + bias[i][h:h + 1, :] for i in range(n)]
        m_old = m_ref[h:h + 1, :]
        m_new = m_old
        for s in scores:
            m_new = jnp.maximum(m_new, s)
        alpha = jnp.exp(m_old - m_new)
        l = alpha * l_ref[h:h + 1, :]
        acc = alpha * acc_ref[hs, :]
        for i, s in enumerate(scores):
            p = jnp.exp(s - m_new)
            l = l + p
            acc = acc + p * v_pages[i][hs, :]
        m_ref[h:h + 1, :], l_ref[h:h + 1, :], acc_ref[hs, :] = m_new, l, acc

    @pl.when(gi == pl.num_programs(1) - 1)
    def _():
        for h, hs in enumerate(heads):
            m = m_ref[h:h + 1, :]
            wgt = jnp.exp(m - jnp.max(m, axis=1, keepdims=True))
            total = jnp.sum(l_ref[h:h + 1, :] * wgt, axis=1, keepdims=True)
            o_ref[hs, :] = jnp.sum(acc_ref[hs, :] * wgt, axis=1, keepdims=True) / total


def _fox_decode_call(layer, qc, knc, vnc, smn3, cache_kt, cache_vt, cache_lf, page_table, suf):
    bs, w, _ = qc.shape
    n_pages = page_table.shape[1]
    n_groups = n_pages // PAGES_PER_STEP
    assert n_pages % PAGES_PER_STEP == 0

    def page(i, rows):
        return _page_spec(layer, rows, lambda b, g, pt: (pt[b, (n_groups - 1 - g) * PAGES_PER_STEP + i], 0, 0))

    col = pl.BlockSpec((None, w, 1), lambda b, g, pt: (b, 0, 0))
    rng = range(PAGES_PER_STEP)
    return pl.pallas_call(
        _fox_decode_kernel,
        grid_spec=pltpu.PrefetchScalarGridSpec(
            num_scalar_prefetch=1,
            grid=(bs, n_groups),
            in_specs=[col, col, col, pl.BlockSpec((None, 1, SMALL_COLS), lambda b, g, pt: (b, 0, 0))]
            + [page(i, w) for i in rng] + [page(i, w) for i in rng] + [page(i, 8) for i in rng]
            + [pl.BlockSpec(suf.shape, lambda b, g, pt: (0, 0))],
            out_specs=col,
            scratch_shapes=[pltpu.VMEM((w, LANES), _F32), pltpu.VMEM((8, LANES), _F32), pltpu.VMEM((8, LANES), _F32),
                            pltpu.VMEM((w, LANES), _F32), pltpu.VMEM((8, 1), _F32)]),
        out_shape=jax.ShapeDtypeStruct((bs, w, 1), _F32),
        compiler_params=_params("parallel", "arbitrary"),
        name="fox_decode",
    )(page_table, qc, knc, vnc, smn3, *([cache_kt] * PAGES_PER_STEP), *([cache_vt] * PAGES_PER_STEP),
      *([cache_lf] * PAGES_PER_STEP), suf)


def _gdn_decode_kernel(x_ref, buf_ref, z_ref, sm_ref, cw_ref, on_ref, s0_ref, o_ref, s_out_ref):
    w = BRANCH_WIDTH
    conv = cw_ref[GDN_CONV - 1:GDN_CONV, :] * x_ref[...]
    for j in range(GDN_CONV - 1):
        conv = conv + cw_ref[j:j + 1, :] * buf_ref[j:j + 1, :]
    act = conv * jax.nn.sigmoid(conv)
    sm = sm_ref[...]
    z = z_ref[...]
    eye = lax.broadcasted_iota(jnp.int32, (HEAD_DIM, HEAD_DIM), 0) == lax.broadcasted_iota(jnp.int32, (HEAD_DIM, HEAD_DIM), 1)
    to_col = lambda r: jnp.sum(jnp.where(eye, r, 0.0), axis=1, keepdims=True)
    l2 = lambda a: a * lax.rsqrt(jnp.sum(a * a, axis=1, keepdims=True) + EPS)
    for h in range(N_HEADS):
        hs = slice(h * HEAD_DIM, (h + 1) * HEAD_DIM)
        qh = l2(act[:, hs]) * (HEAD_DIM ** -0.5)
        kh = l2(act[:, w + h * HEAD_DIM:w + (h + 1) * HEAD_DIM])
        vh = act[:, 2 * w + h * HEAD_DIM:2 * w + (h + 1) * HEAD_DIM]
        beta = sm[:, N_HEADS + h:N_HEADS + h + 1]
        eg = jnp.exp(sm[:, 2 * N_HEADS + h:2 * N_HEADS + h + 1])
        state = s0_ref[h]
        kcol, qcol = to_col(kh), to_col(qh)
        v_new = beta * (vh - eg * jnp.sum(kcol * state, axis=0, keepdims=True))
        o = eg * jnp.sum(qcol * state, axis=0, keepdims=True) + jnp.sum(qh * kh, axis=1, keepdims=True) * v_new
        s_out_ref[h] = state * eg + kcol * v_new
        on = o * lax.rsqrt(jnp.mean(o * o, axis=-1, keepdims=True) + EPS) * on_ref[...]
        zh = z[:, hs]
        o_ref[:, hs] = on * (zh * jax.nn.sigmoid(zh))


def _gdn_decode_call(x3, buf, z3, sm3, conv_w, on_gain, s0):
    bs = x3.shape[0]
    w = BRANCH_WIDTH
    vec = lambda n: pl.BlockSpec((None, 1, n), lambda b: (b, 0, 0))
    st = pl.BlockSpec((None, N_HEADS, HEAD_DIM, HEAD_DIM), lambda b: (b, 0, 0, 0))
    return pl.pallas_call(
        _gdn_decode_kernel,
        grid=(bs,),
        in_specs=[vec(3 * w), pl.BlockSpec((None, GDN_CONV - 1, 3 * w), lambda b: (b, 0, 0)), vec(w), vec(SMALL_COLS),
                  _const_spec(conv_w.shape), _const_spec(on_gain.shape), st],
        out_specs=[vec(w), st],
        out_shape=[jax.ShapeDtypeStruct((bs, 1, w), _F32), jax.ShapeDtypeStruct(s0.shape, _F32)],
        compiler_params=_params("parallel"),
        name="gdn_decode",
    )(x3, buf, z3, sm3, conv_w, on_gain, s0)


def _constants():
    w = BRANCH_WIDTH
    i = jnp.arange(w)
    c = jnp.arange(GDN_CHUNK)
    return dict(
        seg=(i[:, None] // HEAD_DIM == i[None, :] // HEAD_DIM).astype(_BF16),
        tri_blk=(i[:, None] >= i[None, :]).astype(_BF16),
        tri_l=(c[:, None] >= c[None, :]).astype(_BF16),
        ones=jnp.ones((LANES, LANES), _BF16),
        suf=(jnp.arange(LANES)[:, None] > jnp.arange(LANES)[None, :]).astype(_BF16),
        slopes=jnp.broadcast_to(jnp.asarray([_alibi_slope(h) for h in range(N_HEADS)], _F32)[:, None, None],
                                (N_HEADS, 1, LANES)),
    )


def _block_diag(t):
    l, g, a, b = t.shape
    eye = jnp.eye(g, dtype=bool)
    out = jnp.where(eye[None, :, None, :, None], t[:, :, :, None, :], jnp.zeros((), t.dtype))
    return out.reshape(l, g * a, g * b)


def _prep_weights(p):
    depth, d, _ = p["w_in"].shape
    w = BRANCH_WIDTH
    nh = N_HEADS
    w_in = p["w_in"]
    off_fb = 6 * w
    off_s5 = off_fb + nh
    off_beta = off_s5 + w + 4 * w
    off_gate = off_beta + 2 * nh
    wm = jnp.concatenate([w_in[:, :, 0:off_fb], w_in[:, :, off_s5:off_beta]], axis=-1)
    wg = w_in[:, :, off_gate:]
    ws = jnp.concatenate([w_in[:, :, off_fb:off_s5], w_in[:, :, off_beta:off_gate],
                          jnp.zeros((depth, d, SMALL_COLS - 3 * nh), _F32)], axis=-1)
    tile_h = lambda g: jnp.tile(g, (1, nh))
    zeros_w = jnp.zeros((depth, w), _F32)
    gains = jnp.stack([tile_h(p["moba_qn_g"]), tile_h(p["moba_kn_g"]), tile_h(p["fox_qn_g"]), tile_h(p["fox_kn_g"]),
                       zeros_w, zeros_w, zeros_w, zeros_w], axis=1)
    z4 = jnp.zeros((depth, nh), _F32)
    pad = jnp.zeros((depth, SMALL_COLS - 3 * nh), _F32)
    sp_bias = jnp.concatenate([p["fox_f_bias"], z4, p["gdn_dt_bias"], pad], axis=-1)
    sp_alog = jnp.concatenate([z4, z4, p["gdn_a_log"], pad], axis=-1)
    sp = jnp.concatenate([sp_bias[:, None], sp_alog[:, None], jnp.zeros((depth, 6, SMALL_COLS), _F32)], axis=1)
    wt = wm[:, :, 0:6 * w].transpose(0, 2, 1)
    wst = ws[:, :, 0:16].transpose(0, 2, 1)
    gains_t = gains.transpose(0, 2, 1)
    sp_t = jnp.concatenate([sp[:, 0:2, 0:16].transpose(0, 2, 1), jnp.zeros((depth, 16, 6), _F32)], axis=-1)

    g, s = p["s5_lambda_re"].shape[1:]
    flat = lambda a: a.reshape(depth * g, -1)
    ldt = jnp.broadcast_to(p["s5_log_dt"][:, :, None], (depth, g, s))
    lbre, lbim, fre, fim = _whole_call(_s5_disc_kernel, 4, (depth * g, s), "s5_disc",
                                       flat(p["s5_lambda_re"]), flat(p["s5_lambda_im"]), flat(ldt))
    rep = lambda a: jnp.repeat(a, S5_GROUP, axis=-1)
    bbre, bbim = _whole_call(_s5_bbar_kernel, 2, (depth * g, s * S5_GROUP), "s5_bbar",
                             rep(fre), rep(fim), flat(p["s5_b_re"]), flat(p["s5_b_im"]))
    to_hp = lambda a: a.reshape(depth, g, s, S5_GROUP).transpose(0, 1, 3, 2)
    bbd = jnp.concatenate([_block_diag(to_hp(bbre)), _block_diag(to_hp(bbim))], axis=-1)
    to_ph = lambda a: a.transpose(0, 1, 3, 2)
    cre = _block_diag(to_ph(p["s5_c_re"]))
    cim = _block_diag(to_ph(p["s5_c_im"]))

    conv_w = jnp.concatenate([p["gdn_conv_w"], jnp.zeros((depth, 8 - GDN_CONV, 3 * w), _F32)], axis=1)
    out = dict(
        g1=p["norm1_g"][:, None, :], g2=p["norm2_g"][:, None, :], gains=gains, sp=sp,
        wt=wt.astype(_BF16), wst=wst.astype(_BF16), gains_t=gains_t, sp_t=sp_t,
        lbre=lbre.reshape(depth, 1, g * s), lbim=lbim.reshape(depth, 1, g * s),
        dskip=p["s5_d"][:, None, :], conv_w=conv_w, on_gain=p["gdn_out_g"][:, None, :],
    )
    for name, a in dict(wm=wm, wg=wg, ws=ws, bbd=bbd, cre=cre, cim=cim, glu=p["s5_w_glu"], wb=p["w_branch"], wo=p["w_out"],
                        wi=p["mlp_in"], wo2=p["mlp_out"]).items():
        hi = lax.reduce_precision(a, exponent_bits=8, mantissa_bits=7)
        out[name], out[name + "_lo"] = hi.astype(_BF16), (a - hi).astype(_BF16)
    return out


def _layer_prompt(x2d, mod, lw, cst, batch, seq):
    t = batch * seq
    w = BRANCH_WIDTH
    tm = 256
    r3 = lambda a: a.reshape(batch, seq, a.shape[-1])
    qat, kat, vat, qbt, kbt, vbt, kar, kmean, kbaug, u, qkvd, zd, sm, smt = _inproj_prompt_call(
        x2d, mod, lw, cst["seg"], cst["tri_blk"], batch=batch, seq=seq)
    nb = kmean.shape[1]
    kmean = jnp.pad(kmean.reshape(batch, nb, w), ((0, 0), (0, -nb % 8), (0, 0)))
    oa = _moba_prompt_call(qat, kar, vat, kmean)
    ob = _fox_prompt_call(qbt, kbaug, vbt)
    n = lw["lbre"].shape[-1]
    u_tm = r3(u).transpose(1, 0, 2).reshape(t, w)
    h0 = jnp.zeros((batch, n), _F32)
    oc_tm, sre, sim = _s5_call(u_tm, h0, h0, lw["lbre"], lw["lbim"], lw["dskip"], (lw["bbd"],), (lw["cre"],),
                               (lw["cim"],), (lw["glu"],), steps=64)
    oc = oc_tm.reshape(seq, batch, w).transpose(1, 0, 2).reshape(t, w)
    od, gstate = _gdn_prompt_call(r3(qkvd), r3(zd), r3(sm), lw["conv_w"], lw["on_gain"], cst["seg"], cst["tri_l"])
    x1 = _merge_call(oa.reshape(t, w), ob.reshape(t, w), oc, od.reshape(t, w), x2d, mod, lw["g1"], (lw["wg"],),
                     (lw["wb"],), (lw["wo"],), seq=seq, tm=tm)
    x2 = _mlp_call(x1, mod, lw["g2"], (lw["wi"],), (lw["wo2"],), seq=seq, tm=tm)
    groups = n // S5_STATE
    new = dict(moba_k=kat, moba_v=vat, fox_k=kbt, fox_v=vbt, fox_logf=smt[:, 0:N_HEADS, :],
               s5_re=sre.reshape(batch, groups, S5_STATE), s5_im=sim.reshape(batch, groups, S5_STATE),
               gdn=gstate, gdn_conv=r3(qkvd)[:, seq - (GDN_CONV - 1):, :])
    return x2, new


def _layer_sample(x2d, mod, lw, cst, layer, caches, page_table, past):
    bs = x2d.shape[0]
    w = BRANCH_WIDTH
    nh = N_HEADS
    past_len = page_table.shape[1] * LANES
    qa, ka, va, qb, kb, vb, u, qkvd, zd, sm = _inproj_sample_call(x2d, mod, lw, cst["seg"])
    r3 = lambda a: a.reshape(bs, 1, a.shape[-1])
    r4 = lambda a: a.reshape(bs, nh, 1, HEAD_DIM)
    picked = _moba_select_call(layer, r3(qa), caches["moba_k"], page_table, cst["ones"])
    sel = picked[:, 0:nh, 0:MOBA_TOPK].reshape(bs, nh * MOBA_TOPK)
    oa = _moba_decode_call(layer, r4(qa), r4(ka), r4(va), sel, caches["moba_k"], caches["moba_v"], page_table,
                           cst["slopes"], past_len)
    col = lambda a: a.reshape(bs, w, 1)
    ob = _fox_decode_call(layer, col(qb), col(kb), col(vb), r3(sm), caches["fox_k"], caches["fox_v"],
                          caches["fox_logf"], page_table, cst["suf"])
    n = lw["lbre"].shape[-1]
    pair = lambda name: (lw[name], lw[name + "_lo"])
    oc, sre, sim = _s5_call(u, past["s5_re"].reshape(bs, n), past["s5_im"].reshape(bs, n), lw["lbre"], lw["lbim"],
                            lw["dskip"], pair("bbd"), pair("cre"), pair("cim"), pair("glu"), steps=1)
    od, gstate = _gdn_decode_call(r3(qkvd), past["gdn_conv"], r3(zd), r3(sm), lw["conv_w"], lw["on_gain"], past["gdn"])
    x1 = _merge_call(oa.reshape(bs, w), ob.reshape(bs, w), oc, od.reshape(bs, w), x2d, mod, lw["g1"], pair("wg"),
                     pair("wb"), pair("wo"), seq=1, tm=bs)
    x2 = _mlp_call(x1, mod, lw["g2"], pair("wi"), pair("wo2"), seq=1, tm=bs)
    heads = lambda a: a.reshape(bs, 1, nh, HEAD_DIM)
    new = dict(moba_k=heads(ka), moba_v=heads(va), fox_k=heads(kb), fox_v=heads(vb),
               fox_logf=sm[:, 0:nh].reshape(bs, 1, nh),
               s5_re=sre.reshape(past["s5_re"].shape), s5_im=sim.reshape(past["s5_im"].shape), gdn=gstate,
               gdn_conv=jnp.concatenate([past["gdn_conv"][:, 1:], r3(qkvd)], axis=1))
    return x2, new


_STATE_KEYS = ("moba_k", "moba_v", "fox_k", "fox_v", "fox_logf", "s5_re", "s5_im", "gdn", "gdn_conv")


def kernel(x_prompt, x_sample, cache_moba_k, cache_moba_v, cache_fox_k, cache_fox_v, cache_fox_logf, state_s5_re, state_s5_im, state_gdn, state_gdn_conv, page_table, c_prompt, c_sample, norm1_g, norm2_g, ada_w, ada_b, w_in, moba_qn_g, moba_kn_g, fox_qn_g, fox_kn_g, fox_f_bias, s5_lambda_re, s5_lambda_im, s5_b_re, s5_b_im, s5_c_re, s5_c_im, s5_d, s5_log_dt, s5_w_glu, gdn_conv_w, gdn_a_log, gdn_dt_bias, gdn_out_g, w_branch, w_out, mlp_in, mlp_out):
    batch, seq, d = x_prompt.shape
    bs, dec_seq, _ = x_sample.shape
    depth = w_in.shape[0]
    assert dec_seq == 1 and page_table.shape == (bs, page_table.shape[1]) and cache_moba_k.shape[2] == LANES
    assert (batch + bs) % 8 == 0 and batch % 8 == 0 and bs % 8 == 0
    weights = dict(norm1_g=norm1_g, norm2_g=norm2_g, w_in=w_in, moba_qn_g=moba_qn_g, moba_kn_g=moba_kn_g,
                   fox_qn_g=fox_qn_g, fox_kn_g=fox_kn_g, fox_f_bias=fox_f_bias, s5_lambda_re=s5_lambda_re,
                   s5_lambda_im=s5_lambda_im, s5_b_re=s5_b_re, s5_b_im=s5_b_im, s5_c_re=s5_c_re, s5_c_im=s5_c_im,
                   s5_d=s5_d, s5_log_dt=s5_log_dt, s5_w_glu=s5_w_glu, gdn_conv_w=gdn_conv_w, gdn_a_log=gdn_a_log,
                   gdn_dt_bias=gdn_dt_bias, gdn_out_g=gdn_out_g, w_branch=w_branch, w_out=w_out, mlp_in=mlp_in,
                   mlp_out=mlp_out)
    lw_all = _prep_weights(weights)
    cst = _constants()
    mod_all = _ada_call(jnp.concatenate([c_prompt, c_sample], axis=0), ada_w, ada_b)

    pages_t = lambda c: c.transpose(0, 1, 3, 4, 2).reshape(c.shape[0], c.shape[1], BRANCH_WIDTH, LANES)
    logf_t = jnp.pad(cache_fox_logf.transpose(0, 1, 3, 2), ((0, 0), (0, 0), (0, 8 - N_HEADS), (0, 0)))
    caches = dict(moba_k=pages_t(cache_moba_k), moba_v=pages_t(cache_moba_v), fox_k=pages_t(cache_fox_k),
                  fox_v=pages_t(cache_fox_v), fox_logf=logf_t)

    xp = x_prompt.reshape(batch * seq, d)
    xs = x_sample.reshape(bs, d)
    new_p = {k: [] for k in _STATE_KEYS}
    new_s = {k: [] for k in _STATE_KEYS}
    for layer in range(depth):
        lw = {k: v[layer] for k, v in lw_all.items()}
        xp, st_p = _layer_prompt(xp, mod_all[layer, 0:batch][:, None, :], lw, cst, batch, seq)
        past = dict(s5_re=state_s5_re[layer], s5_im=state_s5_im[layer], gdn=state_gdn[layer],
                    gdn_conv=state_gdn_conv[layer])
        xs, st_s = _layer_sample(xs, mod_all[layer, batch:], lw, cst, layer, caches, page_table, past)
        for k in _STATE_KEYS:
            new_p[k].append(st_p[k])
            new_s[k].append(st_s[k])
    pn = {k: jnp.stack(v) for k, v in new_p.items()}
    sn = {k: jnp.stack(v) for k, v in new_s.items()}
    heads_last = lambda a: a.reshape(depth, batch, N_HEADS, HEAD_DIM, seq).transpose(0, 1, 4, 2, 3)
    return (xp.reshape(batch, seq, d), xs.reshape(bs, 1, d),
            heads_last(pn["moba_k"]), heads_last(pn["moba_v"]), heads_last(pn["fox_k"]), heads_last(pn["fox_v"]),
            pn["fox_logf"].transpose(0, 1, 3, 2), pn["s5_re"], pn["s5_im"], pn["gdn"], pn["gdn_conv"],
            sn["moba_k"], sn["moba_v"], sn["fox_k"], sn["fox_v"], sn["fox_logf"],
            sn["s5_re"], sn["s5_im"], sn["gdn"], sn["gdn_conv"])
```
